```python
import jax
import jax.numpy as jnp
from jax import lax
import numpy as np

D_MODEL = 1024
BATCH = 32
SEQ = 2048
DEPTH = 1

HEAD_DIM = 64
FOX_HEADS = D_MODEL // (2 * HEAD_DIM)
NSA_HEADS = D_MODEL // (2 * HEAD_DIM)
NSA_KV_HEADS = max(1, NSA_HEADS // 4)
MIX_WIDTH = (FOX_HEADS + NSA_HEADS) * HEAD_DIM
CMP_LEN = 32
CMP_STRIDE = 16
CMP_HIDDEN = 2 * HEAD_DIM
SEL_BLOCK = 64
SEL_TOPK = 16
WINDOW = 512
Q_BLOCK = 128
SEL_CHUNK = 16
N_BRANCH = 3
MEM_LEN = 256
CROSS_HEADS = 4
CROSS_HEAD_DIM = D_MODEL // CROSS_HEADS
MLP_HIDDEN = 4 * D_MODEL
ROPE_THETA = 10000.0
RMS_EPS = 1e-6
NEG_BIG = -1e30
FORCE_SCORE = 1e4

FOX_QKV = FOX_HEADS * HEAD_DIM
NSA_Q = NSA_HEADS * HEAD_DIM
NSA_KV = NSA_KV_HEADS * HEAD_DIM
IN_SPLITS = (FOX_QKV, FOX_QKV, FOX_QKV, FOX_HEADS, NSA_Q, NSA_KV, NSA_KV, NSA_KV, NSA_KV, NSA_KV, NSA_KV, NSA_HEADS * N_BRANCH)
IN_COLS = sum(IN_SPLITS)

kernel_name = 'hybrid_fox_nsa_sandwich_layer'


def _rms_norm(x, g):
    xf = x.astype(jnp.float32)
    y = xf * lax.rsqrt(jnp.mean(xf * xf, axis=-1, keepdims=True) + RMS_EPS)
    return (y * g.astype(jnp.float32)).astype(x.dtype)


def _rope(x, pos):
    half = x.shape[-1] // 2
    inv = ROPE_THETA ** (-jnp.arange(half, dtype=jnp.float32) / half)
    ang = pos.astype(jnp.float32)[:, None] * inv[None, :]
    cos, sin = jnp.cos(ang), jnp.sin(ang)
    xf = x.astype(jnp.float32)
    x1, x2 = xf[..., :half], xf[..., half:]
    return jnp.concatenate([x1 * cos - x2 * sin, x2 * cos + x1 * sin], axis=-1).astype(x.dtype)


def _split_cols(a, sizes):
    out, lo = [], 0
    for s in sizes:
        out.append(a[..., lo:lo + s])
        lo += s
    return out


def _fox_attention(q, k, v, log_f):
    T = q.shape[2]
    scale = q.shape[-1] ** -0.5
    c = jnp.cumsum(log_f, axis=-1)
    outs = []
    for i in range(T // Q_BLOCK):
        lo, hi = i * Q_BLOCK, (i + 1) * Q_BLOCK
        s = jnp.einsum('bhqd,bhkd->bhqk', q[:, :, lo:hi], k[:, :, :hi]).astype(jnp.float32) * scale
        s = s + c[:, :, lo:hi, None] - c[:, :, None, :hi]
        causal = jnp.arange(lo, hi)[:, None] >= jnp.arange(hi)[None, :]
        s = jnp.where(causal, s, -jnp.inf)
        p = jax.nn.softmax(s, axis=-1).astype(v.dtype)
        outs.append(jnp.einsum('bhqk,bhkd->bhqd', p, v[:, :, :hi]))
    return jnp.concatenate(outs, axis=2)


def _nsa_attention(q, k_cmp, v_cmp, k_slc, v_slc, k_win, v_win, gate_logits,
                   w_ck1, w_ck2, w_cv1, w_cv2, pe_k, pe_v):
    B, T, H, dh = q.shape
    G = k_cmp.shape[1]
    R = H // G
    scale = dh ** -0.5
    pos = jnp.arange(T)
    qg = q.reshape(B, T, G, R, dh).transpose(0, 2, 3, 1, 4)

    n_cmp = (T - CMP_LEN) // CMP_STRIDE + 1
    starts = jnp.arange(n_cmp) * CMP_STRIDE
    blk_idx = starts[:, None] + jnp.arange(CMP_LEN)[None, :]

    def compress(a, pe, w1, w2):
        blk = a[:, :, blk_idx] + pe
        flat = blk.reshape(B, G, n_cmp, CMP_LEN * dh)
        return jax.nn.silu(flat @ w1) @ w2

    kc = compress(k_cmp, pe_k, w_ck1, w_ck2)
    vc = compress(v_cmp, pe_v, w_cv1, w_cv2)
    s_c = jnp.einsum('bgrtd,bgnd->bgrtn', qg, kc).astype(jnp.float32) * scale
    valid_c = (starts + CMP_LEN - 1)[None, :] <= pos[:, None]
    s_c = jnp.where(valid_c, s_c, NEG_BIG)
    p_c = jax.nn.softmax(s_c, axis=-1) * jnp.any(valid_c, axis=-1)[:, None].astype(jnp.float32)
    o_cmp = jnp.einsum('bgrtn,bgnd->bgrtd', p_c.astype(vc.dtype), vc)

    n_sel = T // SEL_BLOCK
    sel_lo = jnp.arange(n_sel) * SEL_BLOCK
    overlap = ((starts[:, None] < sel_lo[None, :] + SEL_BLOCK)
               & (starts[:, None] + CMP_LEN > sel_lo[None, :])).astype(jnp.float32)
    imp = jnp.einsum('bgrtn,nj->bgtj', p_c, overlap)
    cur = pos // SEL_BLOCK
    jb = jnp.arange(n_sel)
    is_cur = jb[None, :] == cur[:, None]
    is_fixed = (jb[None, :] == 0) | (jb[None, :] == cur[:, None] - 1)
    imp = jnp.where(is_cur, 2.0 * FORCE_SCORE, jnp.where(is_fixed, FORCE_SCORE, imp))
    imp = jnp.where(jb[None, :] <= cur[:, None], imp, -1.0)
    n_top = min(SEL_TOPK, n_sel)
    _, top_idx = lax.top_k(imp, n_top)

    q_rot = _rope(qg, pos)
    ks_blk = _rope(k_slc, pos).reshape(B, G, n_sel, SEL_BLOCK, dh)
    vs_blk = v_slc.reshape(B, G, n_sel, SEL_BLOCK, dh)
    n_ch = T // SEL_CHUNK
    q_ch = q_rot.reshape(B, G, R, n_ch, SEL_CHUNK, dh).transpose(3, 0, 1, 2, 4, 5)
    i_ch = top_idx.reshape(B, G, n_ch, SEL_CHUNK, n_top).transpose(2, 0, 1, 3, 4)
    t_ch = pos.reshape(n_ch, SEL_CHUNK)
    bi = jnp.arange(B)[:, None, None, None]
    gi = jnp.arange(G)[None, :, None, None]
    n_keys = n_top * SEL_BLOCK

    def sel_step(args):
        qc, ic, tc = args
        kg = ks_blk[bi, gi, ic].reshape(B, G, SEL_CHUNK, n_keys, dh)
        vg = vs_blk[bi, gi, ic].reshape(B, G, SEL_CHUNK, n_keys, dh)
        kpos = (ic[..., None] * SEL_BLOCK + jnp.arange(SEL_BLOCK)).reshape(B, G, SEL_CHUNK, n_keys)
        mask = kpos <= tc[:, None]
        s = jnp.einsum('bgrcd,bgckd->bgrck', qc, kg).astype(jnp.float32) * scale
        s = jnp.where(mask[:, :, None], s, -jnp.inf)
        p = jax.nn.softmax(s, axis=-1).astype(vg.dtype)
        return jnp.einsum('bgrck,bgckd->bgrcd', p, vg)

    o_slc = lax.map(sel_step, (q_ch, i_ch, t_ch))
    o_slc = o_slc.transpose(1, 2, 3, 0, 4, 5).reshape(B, G, R, T, dh)

    pad = ((0, 0), (0, 0), (WINDOW, 0), (0, 0))
    kw = jnp.pad(_rope(k_win, pos), pad)
    vw = jnp.pad(v_win, pad)
    n_qb = T // Q_BLOCK
    q_blk = q_rot.reshape(B, G, R, n_qb, Q_BLOCK, dh).transpose(3, 0, 1, 2, 4, 5)

    def win_step(args):
        qb, i = args
        lo = i * Q_BLOCK
        kb = lax.dynamic_slice_in_dim(kw, lo, Q_BLOCK + WINDOW, axis=2)
        vb = lax.dynamic_slice_in_dim(vw, lo, Q_BLOCK + WINDOW, axis=2)
        qpos = lo + jnp.arange(Q_BLOCK)
        kpos = lo - WINDOW + jnp.arange(Q_BLOCK + WINDOW)
        mask = ((kpos[None, :] <= qpos[:, None]) & (kpos[None, :] > qpos[:, None] - WINDOW)
                & (kpos[None, :] >= 0))
        s = jnp.einsum('bgrqd,bgkd->bgrqk', qb, kb).astype(jnp.float32) * scale
        s = jnp.where(mask, s, -jnp.inf)
        p = jax.nn.softmax(s, axis=-1).astype(vb.dtype)
        return jnp.einsum('bgrqk,bgkd->bgrqd', p, vb)

    o_win = lax.map(win_step, (q_blk, jnp.arange(n_qb)))
    o_win = o_win.transpose(1, 2, 3, 0, 4, 5).reshape(B, G, R, T, dh)

    gts = jax.nn.sigmoid(gate_logits.astype(jnp.float32)).astype(q.dtype)
    gts = gts.reshape(B, T, G, R, N_BRANCH).transpose(0, 2, 3, 1, 4)[..., None]
    o = gts[..., 0, :] * o_cmp + gts[..., 1, :] * o_slc + gts[..., 2, :] * o_win
    return o.transpose(0, 3, 1, 2, 4).reshape(B, T, H * dh)


def _hybrid_mixer(n, w_in, b_forget, w_ck1, w_ck2, w_cv1, w_cv2, pe_k, pe_v, w_out):
    B, T, _ = n.shape
    proj = n @ w_in
    fq, fk, fv, ff, nq, kc, vc, ks, vs, kw, vw, ng = _split_cols(proj, IN_SPLITS)

    def heads(a, h):
        return a.reshape(B, T, h, HEAD_DIM).transpose(0, 2, 1, 3)

    log_f = jax.nn.log_sigmoid((ff + b_forget).astype(jnp.float32)).transpose(0, 2, 1)
    o_fox = _fox_attention(heads(fq, FOX_HEADS), heads(fk, FOX_HEADS), heads(fv, FOX_HEADS), log_f)
    o_fox = o_fox.transpose(0, 2, 1, 3).reshape(B, T, FOX_QKV)
    o_nsa = _nsa_attention(nq.reshape(B, T, NSA_HEADS, HEAD_DIM),
                           heads(kc, NSA_KV_HEADS), heads(vc, NSA_KV_HEADS),
                           heads(ks, NSA_KV_HEADS), heads(vs, NSA_KV_HEADS),
                           heads(kw, NSA_KV_HEADS), heads(vw, NSA_KV_HEADS),
                           ng.reshape(B, T, NSA_HEADS, N_BRANCH),
                           w_ck1, w_ck2, w_cv1, w_cv2, pe_k, pe_v)
    return jnp.concatenate([o_fox, o_nsa], axis=-1) @ w_out


def _memory_cross_attention(n, m, w_q, w_kv, w_o):
    B, T, D = n.shape
    M = m.shape[1]
    q = (n @ w_q).reshape(B, T, CROSS_HEADS, CROSS_HEAD_DIM)
    kv = (m @ w_kv).reshape(B, M, 2, CROSS_HEADS, CROSS_HEAD_DIM)
    k, v = kv[:, :, 0], kv[:, :, 1]
    s = jnp.einsum('bthd,bmhd->bhtm', q, k).astype(jnp.float32) * (CROSS_HEAD_DIM ** -0.5)
    p = jax.nn.softmax(s, axis=-1).astype(v.dtype)
    o = jnp.einsum('bhtm,bmhd->bthd', p, v).reshape(B, T, D)
    return o @ w_o


def _sq_relu_mlp(n, w_up, w_down):
    return jnp.square(jax.nn.relu(n @ w_up)) @ w_down


def setup_inputs(seed: int = 0) -> dict:
    key = jax.random.key(seed)
    ks = jax.random.split(key, 23)
    f32 = jnp.float32
    L = DEPTH

    def dense(k, shape, fan_in):
        return jax.random.normal(k, shape, f32) * fan_in ** -0.5

    def gain(k, dim):
        return 1.0 + 0.05 * jax.random.normal(k, (L, dim), f32)

    return {
        'x': jax.random.normal(ks[0], (BATCH, SEQ, D_MODEL), f32),
        'mem': jax.random.normal(ks[1], (BATCH, MEM_LEN, D_MODEL), f32),
        'g_mix_pre': gain(ks[2], D_MODEL),
        'w_in': dense(ks[3], (L, D_MODEL, IN_COLS), D_MODEL),
        'b_forget': jax.random.uniform(ks[4], (L, FOX_HEADS), f32, 1.0, 5.0),
        'w_ck1': dense(ks[5], (L, CMP_LEN * HEAD_DIM, CMP_HIDDEN), CMP_LEN * HEAD_DIM),
        'w_ck2': dense(ks[6], (L, CMP_HIDDEN, HEAD_DIM), CMP_HIDDEN),
        'w_cv1': dense(ks[7], (L, CMP_LEN * HEAD_DIM, CMP_HIDDEN), CMP_LEN * HEAD_DIM),
        'w_cv2': dense(ks[8], (L, CMP_HIDDEN, HEAD_DIM), CMP_HIDDEN),
        'pe_k': 0.1 * jax.random.normal(ks[9], (L, CMP_LEN, HEAD_DIM), f32),
        'pe_v': 0.1 * jax.random.normal(ks[10], (L, CMP_LEN, HEAD_DIM), f32),
        'w_mix_out': dense(ks[11], (L, MIX_WIDTH, D_MODEL), MIX_WIDTH),
        'g_mix_post': gain(ks[12], D_MODEL),
        'g_x_pre': gain(ks[13], D_MODEL),
        'g_mem': gain(ks[14], D_MODEL),
        'w_xq': dense(ks[15], (L, D_MODEL, D_MODEL), D_MODEL),
        'w_xkv': dense(ks[16], (L, D_MODEL, 2 * D_MODEL), D_MODEL),
        'w_xo': dense(ks[17], (L, D_MODEL, D_MODEL), D_MODEL),
        'g_x_post': gain(ks[18], D_MODEL),
        'g_mlp_pre': gain(ks[19], D_MODEL),
        'w_up': dense(ks[20], (L, D_MODEL, MLP_HIDDEN), D_MODEL),
        'w_down': dense(ks[21], (L, MLP_HIDDEN, D_MODEL), MLP_HIDDEN),
        'g_mlp_post': gain(ks[22], D_MODEL),
    }


def reference(x, mem, g_mix_pre, w_in, b_forget, w_ck1, w_ck2, w_cv1, w_cv2, pe_k, pe_v,
              w_mix_out, g_mix_post, g_x_pre, g_mem, w_xq, w_xkv, w_xo, g_x_post,
              g_mlp_pre, w_up, w_down, g_mlp_post):
    h = x
    for l in range(DEPTH):
        n = _rms_norm(h, g_mix_pre[l])
        mix = _hybrid_mixer(n, w_in[l], b_forget[l], w_ck1[l], w_ck2[l], w_cv1[l], w_cv2[l],
                            pe_k[l], pe_v[l], w_mix_out[l])
        h = h + _rms_norm(mix, g_mix_post[l])
        n = _rms_norm(h, g_x_pre[l])
        m = _rms_norm(mem, g_mem[l])
        h = h + _rms_norm(_memory_cross_attention(n, m, w_xq[l], w_xkv[l], w_xo[l]), g_x_post[l])
        n = _rms_norm(h, g_mlp_pre[l])
        h = h + _rms_norm(_sq_relu_mlp(n, w_up[l], w_down[l]), g_mlp_post[l])
    return h
```

```python
import functools

import jax
import jax.numpy as jnp
from jax import lax
from jax.experimental import pallas as pl
from jax.experimental.pallas import tpu as pltpu

D_MODEL = 1024
HEAD_DIM = 64
FOX_HEADS = 8
NSA_HEADS = 8
NSA_KV_HEADS = 2
NSA_REP = NSA_HEADS // NSA_KV_HEADS
CMP_LEN = 32
CMP_STRIDE = 16
CMP_HIDDEN = 2 * HEAD_DIM
SEL_BLOCK = 64
SEL_TOPK = 16
WINDOW = 512
N_BRANCH = 3
CROSS_HEADS = 4
CROSS_HEAD_DIM = D_MODEL // CROSS_HEADS
MLP_HIDDEN = 4 * D_MODEL
ROPE_THETA = 10000.0
RMS_EPS = 1e-6
FORCE_SCORE = 1e4
MASKED = -1e30

FOX_QKV = FOX_HEADS * HEAD_DIM
NSA_Q = NSA_HEADS * HEAD_DIM
NSA_KV = NSA_KV_HEADS * HEAD_DIM
AUG = 128

V7X_VMEM_LIMIT = 56 * 1024 * 1024

F32 = jnp.float32
BF16 = jnp.bfloat16


def _params(sem, vmem=V7X_VMEM_LIMIT):
    return pltpu.CompilerParams(dimension_semantics=sem, vmem_limit_bytes=vmem)


def _rms(x, g):
    return x * lax.rsqrt(jnp.mean(x * x, axis=-1, keepdims=True) + RMS_EPS) * g


def _dot(a, b):
    return jnp.dot(a, b, preferred_element_type=F32)


def _dot_nt(a, b):
    return lax.dot_general(a, b, (((1,), (1,)), ((), ())), preferred_element_type=F32)


def _split3(x):
    hi = x.astype(BF16)
    r1 = x - hi.astype(F32)
    mid = r1.astype(BF16)
    lo = (r1 - mid.astype(F32)).astype(BF16)
    return hi, mid, lo


_R_FQ, _R_FK, _R_FV, _R_NQ = 0, 512, 1024, 1536
_R_KC, _R_VC, _R_KS, _R_VS, _R_KW, _R_VW = 2048, 2176, 2304, 2432, 2560, 2688
_R_FF, _R_NG, _R_END = 2816, 2824, 2848


def _in_proj_kernel(x_ref, g_ref, wt_ref, bf_ref, cos_ref, sin_ref,
                    fq_ref, fk_ref, fv_ref, nq_ref, nqr_ref, kcvc_ref,
                    ks_ref, vs_ref, kw_ref, vw_ref, lf_ref, gt_ref):
    n = _rms(x_ref[0], g_ref[...]).astype(BF16)
    cos = cos_ref[...]
    sin = sin_ref[...]
    half = HEAD_DIM // 2

    def proj(lo, hi):
        return _dot_nt(wt_ref[lo:hi, :], n)

    def store_rope(r, out_ref, heads):
        for h in range(heads):
            x1 = r[h * HEAD_DIM:h * HEAD_DIM + half]
            x2 = r[h * HEAD_DIM + half:(h + 1) * HEAD_DIM]
            out_ref[0, h * HEAD_DIM:h * HEAD_DIM + half, :] = (x1 * cos - x2 * sin).astype(BF16)
            out_ref[0, h * HEAD_DIM + half:(h + 1) * HEAD_DIM, :] = (x2 * cos + x1 * sin).astype(BF16)

    scale = HEAD_DIM ** -0.5
    fq_ref[0] = (proj(_R_FQ, _R_FK) * scale).astype(BF16)
    fk_ref[0] = proj(_R_FK, _R_FV).astype(BF16)
    fv_ref[0] = proj(_R_FV, _R_NQ).astype(BF16)
    nq = proj(_R_NQ, _R_KC) * scale
    nq_ref[0] = nq.astype(BF16)
    store_rope(nq, nqr_ref, NSA_HEADS)
    kcvc_ref[0] = proj(_R_KC, _R_KS).astype(BF16)
    store_rope(proj(_R_KS, _R_VS), ks_ref, NSA_KV_HEADS)
    vs_ref[0] = proj(_R_VS, _R_KW).astype(BF16)
    store_rope(proj(_R_KW, _R_VW), kw_ref, NSA_KV_HEADS)
    vw_ref[0] = proj(_R_VW, _R_FF).astype(BF16)
    small = proj(_R_FF, _R_END)
    z = small[0:FOX_HEADS] + bf_ref[...]
    lf_ref[0] = jnp.minimum(z, 0.0) - jnp.log1p(jnp.exp(-jnp.abs(z)))
    gt_ref[0] = jax.nn.sigmoid(small[FOX_HEADS:])


def _in_proj(x, g, wt, b_forget, cos_t, sin_t, tm=512):
    B, T, D = x.shape
    grid = (B, T // tm)
    fm = lambda c: pl.BlockSpec((1, c, tm), lambda b, i: (b, 0, i))
    out_shape = [jax.ShapeDtypeStruct((B, c, T), dt) for c, dt in
                 [(512, BF16)] * 5 + [(256, BF16)] + [(128, BF16)] * 4 + [(8, F32), (24, F32)]]
    out_specs = [fm(s.shape[1]) for s in out_shape]
    return pl.pallas_call(
        _in_proj_kernel,
        grid=grid,
        in_specs=[
            pl.BlockSpec((1, tm, D), lambda b, i: (b, i, 0)),
            pl.BlockSpec((1, D), lambda b, i: (0, 0)),
            pl.BlockSpec((_R_END, D), lambda b, i: (0, 0)),
            pl.BlockSpec((FOX_HEADS, 1), lambda b, i: (0, 0)),
            pl.BlockSpec((HEAD_DIM // 2, tm), lambda b, i: (0, i)),
            pl.BlockSpec((HEAD_DIM // 2, tm), lambda b, i: (0, i)),
        ],
        out_specs=out_specs,
        out_shape=out_shape,
        compiler_params=_params(("parallel", "parallel")),
        name="in_proj",
    )(x, g, wt, b_forget, cos_t, sin_t)


_CS = 128


def _cumsum_kernel(x_ref, o_ref):
    x = x_ref[0]
    r = lax.broadcasted_iota(jnp.int32, (_CS, _CS), 0)
    c = lax.broadcasted_iota(jnp.int32, (_CS, _CS), 1)
    tri = (r <= c).astype(BF16)
    xh, xm, xl = _split3(x)
    y = _dot(xh, tri) + _dot(xm, tri) + _dot(xl, tri)
    sh = (_CS // FOX_HEADS).bit_length() - 1
    prev = ((jnp.right_shift(r, sh) == jnp.right_shift(c, sh)) & (c < r)).astype(BF16)
    tot = jnp.broadcast_to(y[:, _CS - 1:_CS], (_CS, _CS))
    th, tm_, tl = _split3(tot)
    carry = _dot(prev, th) + _dot(prev, tm_) + _dot(prev, tl)
    ch, cm, cl = _split3(y + carry)
    o_ref[0, 0] = ch
    o_ref[0, 1] = cm
    o_ref[0, 2] = cl


def _decay_cumsum(lf):
    B, H, T = lf.shape
    assert H * (T // _CS) == _CS
    x = lf.reshape(B, _CS, _CS)
    out = pl.pallas_call(
        _cumsum_kernel,
        grid=(B,),
        in_specs=[pl.BlockSpec((1, _CS, _CS), lambda b: (b, 0, 0))],
        out_specs=pl.BlockSpec((1, 3, _CS, _CS), lambda b: (b, 0, 0, 0)),
        out_shape=jax.ShapeDtypeStruct((B, 3, _CS, _CS), BF16),
        compiler_params=_params(("parallel",)),
        name="decay_cumsum",
    )(x)
    return out.reshape(B, 3, H, T)


def _flash_kernel(q_ref, k_ref, v_ref, g_ref, o_ref, *, tq, tk, window):
    i = pl.program_id(2)
    q = q_ref[0, 0]
    hd = v_ref.shape[3]

    def step(j, carry, masked):
        m, l, acc = carry
        k = k_ref[0, 0, pl.ds(pl.multiple_of(j * tk, tk), tk), :]
        s = _dot(k, q)
        if masked:
            kpos = j * tk + lax.broadcasted_iota(jnp.int32, (tk, tq), 0)
            qpos = i * tq + lax.broadcasted_iota(jnp.int32, (tk, tq), 1)
            ok = kpos <= qpos
            if window is not None:
                ok = ok & (kpos > qpos - window)
            s = jnp.where(ok, s, MASKED)
        m_new = jnp.maximum(m, jnp.max(s, axis=0, keepdims=True))
        alpha = jnp.exp(m - m_new)
        p = jnp.exp(s - m_new)
        l = alpha * l + jnp.sum(p, axis=0, keepdims=True)
        acc = alpha * acc + _dot(v_ref[0, 0, j], p.astype(BF16))
        return m_new, l, acc

    carry = (jnp.full((1, tq), MASKED, F32), jnp.zeros((1, tq), F32), jnp.zeros((hd, tq), F32))
    j_end = ((i + 1) * tq) // tk
    if window is None:
        j_diag = (i * tq) // tk
        carry = lax.fori_loop(0, j_diag, functools.partial(step, masked=False), carry)
        carry = lax.fori_loop(j_diag, j_end, functools.partial(step, masked=True), carry)
    else:
        j_lo = jnp.maximum(i * tq - (window - 1), 0) // tk
        carry = lax.fori_loop(j_lo, j_end, functools.partial(step, masked=True), carry)
    m, l, acc = carry
    o_ref[0, 0] = (acc * (g_ref[0, 0] / l)).astype(o_ref.dtype)


def _flash(qa_t, ka, v_tc, gate, *, tq, window=None):
    B, H, _, T = qa_t.shape
    G, nk, hd, tk = v_tc.shape[1:]
    rep = H // G
    return pl.pallas_call(
        functools.partial(_flash_kernel, tq=tq, tk=tk, window=window),
        grid=(B, H, T // tq),
        in_specs=[
            pl.BlockSpec((1, 1, AUG, tq), lambda b, h, i: (b, h, 0, i)),
            pl.BlockSpec((1, 1, T, AUG), lambda b, h, i: (b, h // rep, 0, 0)),
            pl.BlockSpec((1, 1, nk, hd, tk), lambda b, h, i: (b, h // rep, 0, 0, 0)),
            pl.BlockSpec((1, 1, 1, tq), lambda b, h, i: (b, h, 0, i)),
        ],
        out_specs=pl.BlockSpec((1, 1, hd, tq), lambda b, h, i: (b, h, 0, i)),
        out_shape=jax.ShapeDtypeStruct((B, H, hd, T), BF16),
        compiler_params=_params(("parallel", "parallel", "parallel")),
        name="flash_window" if window is not None else "flash_causal",
    )(qa_t, ka, v_tc, gate)


def _compress_kernel(x_ref, w1_ref, pe_ref, w2k_ref, w2vt_ref, kc_ref, vct_ref):
    half = CMP_STRIDE * HEAD_DIM
    n_chunk = x_ref.shape[2]
    for kind in range(2):
        w1 = w1_ref[kind]
        pe = pe_ref[kind].astype(BF16)
        bias = (_dot(pe[:, :half], w1[:, :CMP_HIDDEN]) + _dot(pe[:, half:], w1[:, CMP_HIDDEN:]))[0:1]
        for g in range(NSA_KV_HEADS):
            ab = _dot(x_ref[0, kind * NSA_KV_HEADS + g], w1)
            nxt = pltpu.roll(ab[:, CMP_HIDDEN:], n_chunk - 1, 0)
            pre = ab[:, :CMP_HIDDEN] + nxt + bias
            hdn = (pre * jax.nn.sigmoid(pre)).astype(BF16)
            if kind == 0:
                kc_ref[0, g] = _dot(hdn, w2k_ref[...]).astype(BF16)
            else:
                vct_ref[0, g] = _dot_nt(w2vt_ref[...], hdn).astype(BF16)


def _compress(xc, w1cat, pe8, w2k, w2vt):
    B = xc.shape[0]
    n_chunk = xc.shape[2]
    full = lambda a: pl.BlockSpec(a.shape, lambda b: (0,) * a.ndim)
    return pl.pallas_call(
        _compress_kernel,
        grid=(B,),
        in_specs=[pl.BlockSpec((1,) + xc.shape[1:], lambda b: (b, 0, 0, 0)),
                  full(w1cat), full(pe8), full(w2k), full(w2vt)],
        out_specs=[pl.BlockSpec((1, NSA_KV_HEADS, n_chunk, HEAD_DIM), lambda b: (b, 0, 0, 0)),
                   pl.BlockSpec((1, NSA_KV_HEADS, HEAD_DIM, n_chunk), lambda b: (b, 0, 0, 0))],
        out_shape=[jax.ShapeDtypeStruct((B, NSA_KV_HEADS, n_chunk, HEAD_DIM), BF16),
                   jax.ShapeDtypeStruct((B, NSA_KV_HEADS, HEAD_DIM, n_chunk), BF16)],
        compiler_params=_params(("parallel",)),
        name="nsa_compress",
    )(xc, w1cat, pe8, w2k, w2vt)


def _cmp_attn_kernel(q_ref, kc_ref, vct_ref, g_ref, o_ref, sb_ref, *, tq):
    i = pl.program_id(2)
    n_cmp = kc_ref.shape[2]
    n_sel = sb_ref.shape[2]
    n_io = lax.broadcasted_iota(jnp.int32, (n_cmp, tq), 0)
    t_io = i * tq + lax.broadcasted_iota(jnp.int32, (n_cmp, tq), 1)
    valid = n_io * CMP_STRIDE + (CMP_LEN - 1) <= t_io
    kc = kc_ref[0, 0]
    vct = vct_ref[0, 0]
    psum = jnp.zeros((n_cmp, tq), F32)
    for r in range(NSA_REP):
        q = q_ref[0, r * HEAD_DIM:(r + 1) * HEAD_DIM, :]
        s = jnp.where(valid, _dot(kc, q), MASKED)
        m = jnp.max(s, axis=0, keepdims=True)
        e = jnp.where(valid, jnp.exp(s - m), 0.0)
        l = jnp.sum(e, axis=0, keepdims=True)
        p = e * jnp.where(l > 0.0, 1.0 / l, 0.0)
        psum = psum + p
        o = _dot(vct, p.astype(BF16))
        o_ref[0, r * HEAD_DIM:(r + 1) * HEAD_DIM, :] = (o * g_ref[0, 0, 0, r:r + 1, :]).astype(BF16)

    jr = lax.broadcasted_iota(jnp.int32, (n_sel, n_cmp), 0)
    nc = lax.broadcasted_iota(jnp.int32, (n_sel, n_cmp), 1)
    overlap = ((nc * CMP_STRIDE < (jr + 1) * SEL_BLOCK)
               & (nc * CMP_STRIDE + CMP_LEN > jr * SEL_BLOCK)).astype(BF16)
    ph, pm, pl_ = _split3(psum)
    imp = _dot(overlap, ph) + _dot(overlap, pm) + _dot(overlap, pl_)
    j_io = lax.broadcasted_iota(jnp.int32, (n_sel, tq), 0)
    cur = jnp.right_shift(i * tq + lax.broadcasted_iota(jnp.int32, (n_sel, tq), 1),
                          SEL_BLOCK.bit_length() - 1)
    is_cur = j_io == cur
    is_fixed = (j_io == 0) | (j_io == cur - 1)
    imp = jnp.where(is_cur, 2.0 * FORCE_SCORE, jnp.where(is_fixed, FORCE_SCORE, imp))
    imp = jnp.where(j_io <= cur, imp, -1.0)
    cnt = jnp.zeros((n_sel, tq), jnp.int32)
    for jp in range(n_sel):
        row = imp[jp:jp + 1, :]
        beats = (row > imp) | ((row == imp) & (j_io > jp))
        cnt = cnt + beats.astype(jnp.int32)
    sb_ref[0, 0] = jnp.where(cnt < min(SEL_TOPK, n_sel), 0.0, MASKED).astype(BF16)


def _cmp_attn(nq_t, kc, vct, gate_cmp, n_sel, tq=256):
    B, _, T = nq_t.shape
    G = kc.shape[1]
    rows = NSA_REP * HEAD_DIM
    return pl.pallas_call(
        functools.partial(_cmp_attn_kernel, tq=tq),
        grid=(B, G, T // tq),
        in_specs=[
            pl.BlockSpec((1, rows, tq), lambda b, g, i: (b, g, i)),
            pl.BlockSpec((1, 1) + kc.shape[2:], lambda b, g, i: (b, g, 0, 0)),
            pl.BlockSpec((1, 1) + vct.shape[2:], lambda b, g, i: (b, g, 0, 0)),
            pl.BlockSpec((1, 1, 1, NSA_REP, tq), lambda b, g, i: (b, 0, g, 0, i)),
        ],
        out_specs=[pl.BlockSpec((1, rows, tq), lambda b, g, i: (b, g, i)),
                   pl.BlockSpec((1, 1, n_sel, tq), lambda b, g, i: (b, g, 0, i))],
        out_shape=[jax.ShapeDtypeStruct((B, G * rows, T), BF16),
                   jax.ShapeDtypeStruct((B, G, n_sel, T), BF16)],
        compiler_params=_params(("parallel", "parallel", "parallel")),
        name="nsa_cmp_attn",
    )(nq_t, kc, vct, gate_cmp)


def _mix_out_kernel(x_ref, of_ref, oc_ref, os_ref, ow_ref, wo_ref, g_ref, o_ref):
    nsa = (oc_ref[0].astype(F32) + os_ref[0].astype(F32) + ow_ref[0].astype(F32)).astype(BF16)
    mix = _dot(of_ref[0], wo_ref[0:FOX_QKV, :]) + _dot(nsa, wo_ref[FOX_QKV:, :])
    o_ref[0] = x_ref[0] + _rms(mix, g_ref[...])


def _mix_out(x, o_fox, o_cmp, o_slc, o_win, wo, g, tm=512):
    B, T, D = x.shape
    row = lambda c: pl.BlockSpec((1, tm, c), lambda b, i: (b, i, 0))
    return pl.pallas_call(
        _mix_out_kernel,
        grid=(B, T // tm),
        in_specs=[row(D), row(FOX_QKV), row(NSA_Q), row(NSA_Q), row(NSA_Q),
                  pl.BlockSpec(wo.shape, lambda b, i: (0, 0)),
                  pl.BlockSpec((1, D), lambda b, i: (0, 0))],
        out_specs=row(D),
        out_shape=jax.ShapeDtypeStruct((B, T, D), F32),
        compiler_params=_params(("parallel", "parallel")),
        name="mix_out",
    )(x, o_fox, o_cmp, o_slc, o_win, wo, g)


def _mem_kv_kernel(m_ref, g_ref, w_ref, k_ref, v_ref):
    m = _rms(m_ref[0], g_ref[...]).astype(BF16)
    k_ref[0] = _dot(m, w_ref[:, :D_MODEL]).astype(BF16)
    v_ref[0] = _dot(m, w_ref[:, D_MODEL:]).astype(BF16)


def _mem_kv(mem, g, wkv):
    B, M, D = mem.shape
    blk = pl.BlockSpec((1, M, D), lambda b: (b, 0, 0))
    return pl.pallas_call(
        _mem_kv_kernel,
        grid=(B,),
        in_specs=[blk, pl.BlockSpec((1, D), lambda b: (0, 0)), pl.BlockSpec(wkv.shape, lambda b: (0, 0))],
        out_specs=[blk, blk],
        out_shape=[jax.ShapeDtypeStruct((B, M, D), BF16)] * 2,
        compiler_params=_params(("parallel",)),
        name="mem_kv",
    )(mem, g, wkv)


def _cross_kernel(h_ref, k_ref, v_ref, wq_ref, wo_ref, gpre_ref, gpost_ref, o_ref):
    h = h_ref[0]
    n = _rms(h, gpre_ref[...]).astype(BF16)
    scale = CROSS_HEAD_DIM ** -0.5
    acc = jnp.zeros(h.shape, F32)
    for hh in range(CROSS_HEADS):
        sl = slice(hh * CROSS_HEAD_DIM, (hh + 1) * CROSS_HEAD_DIM)
        q = (_dot(n, wq_ref[:, sl]) * scale).astype(BF16)
        s = _dot_nt(q, k_ref[0, :, sl])
        m = jnp.max(s, axis=-1, keepdims=True)
        p = jnp.exp(s - m)
        l = jnp.sum(p, axis=-1, keepdims=True)
        o = _dot(p.astype(BF16), v_ref[0, :, sl]) / l
        acc = acc + _dot(o.astype(BF16), wo_ref[sl, :])
    o_ref[0] = h + _rms(acc, gpost_ref[...])


def _cross(h, k, v, wq, wo, g_pre, g_post, tm=256):
    B, T, D = h.shape
    M = k.shape[1]
    row = pl.BlockSpec((1, tm, D), lambda b, i: (b, i, 0))
    kvb = pl.BlockSpec((1, M, D), lambda b, i: (b, 0, 0))
    wsp = pl.BlockSpec((D, D), lambda b, i: (0, 0))
    gsp = pl.BlockSpec((1, D), lambda b, i: (0, 0))
    return pl.pallas_call(
        _cross_kernel,
        grid=(B, T // tm),
        in_specs=[row, kvb, kvb, wsp, wsp, gsp, gsp],
        out_specs=row,
        out_shape=jax.ShapeDtypeStruct((B, T, D), F32),
        compiler_params=_params(("parallel", "parallel")),
        name="mem_cross",
    )(h, k, v, wq, wo, g_pre, g_post)


def _mlp_kernel(h_ref, wu_ref, wd_ref, gpre_ref, gpost_ref, o_ref, *, hc):
    h = h_ref[...]
    n = _rms(h, gpre_ref[...]).astype(BF16)
    acc = jnp.zeros(h.shape, F32)
    for c in range(wu_ref.shape[1] // hc):
        u = jnp.maximum(_dot(n, wu_ref[:, c * hc:(c + 1) * hc]), 0.0)
        acc = acc + _dot((u * u).astype(BF16), wd_ref[c * hc:(c + 1) * hc, :])
    o_ref[...] = h + _rms(acc, gpost_ref[...])


def _mlp(h, wu, wd, g_pre, g_post, tm=512, hc=512):
    N, D = h.shape
    row = pl.BlockSpec((tm, D), lambda i: (i, 0))
    gsp = pl.BlockSpec((1, D), lambda i: (0, 0))
    once = pl.Buffered(1)
    return pl.pallas_call(
        functools.partial(_mlp_kernel, hc=hc),
        grid=(N // tm,),
        in_specs=[row,
                  pl.BlockSpec(wu.shape, lambda i: (0, 0), pipeline_mode=once),
                  pl.BlockSpec(wd.shape, lambda i: (0, 0), pipeline_mode=once),
                  gsp, gsp],
        out_specs=row,
        out_shape=jax.ShapeDtypeStruct((N, D), F32),
        compiler_params=_params(("parallel",)),
        name="relu2_mlp",
    )(h, wu, wd, g_pre, g_post)


def _layer(h, mem, g_mix_pre, w_in, b_forget, w_ck1, w_ck2, w_cv1, w_cv2, pe_k, pe_v,
           w_mix_out, g_mix_post, g_x_pre, g_mem, w_xq, w_xkv, w_xo, g_x_post,
           g_mlp_pre, w_up, w_down, g_mlp_post):
    B, T, D = h.shape
    H, G, hd = NSA_HEADS, NSA_KV_HEADS, HEAD_DIM
    row = lambda g: g.reshape(1, -1)

    cols = {}
    lo = 0
    for name, size in (("fq", FOX_QKV), ("fk", FOX_QKV), ("fv", FOX_QKV), ("ff", FOX_HEADS), ("nq", NSA_Q),
                       ("kc", NSA_KV), ("vc", NSA_KV), ("ks", NSA_KV), ("vs", NSA_KV), ("kw", NSA_KV),
                       ("vw", NSA_KV), ("ng", NSA_HEADS * N_BRANCH)):
        cols[name] = w_in[:, lo:lo + size]
        lo += size
    ng_branch_major = cols["ng"].reshape(D, NSA_HEADS, N_BRANCH).transpose(0, 2, 1).reshape(D, -1)
    wt = jnp.concatenate([cols[k] for k in ("fq", "fk", "fv", "nq", "kc", "vc", "ks", "vs", "kw", "vw", "ff")]
                         + [ng_branch_major], axis=1).T.astype(BF16)
    half = hd // 2
    inv = ROPE_THETA ** (-jnp.arange(half, dtype=F32) / half)
    ang = inv[:, None] * jnp.arange(T, dtype=F32)[None, :]
    cos_t, sin_t = jnp.cos(ang), jnp.sin(ang)

    (fq_t, fk_t, fv_t, nq_t, nqr_t, kcvc_t, ks_t, vs_t, kw_t, vw_t, lf_t, gt_t) = _in_proj(
        h, row(g_mix_pre), wt, b_forget.reshape(FOX_HEADS, 1), cos_t, sin_t)

    tk = 256
    nk = T // tk

    def key_major(a_t, heads):
        return a_t.reshape(B, heads, hd, T).transpose(0, 1, 3, 2)

    def val_tiles(a_t, heads, tk_):
        return a_t.reshape(B, heads, hd, T // tk_, tk_).transpose(0, 1, 3, 2, 4)

    c3 = _decay_cumsum(lf_t)
    c3h = c3.transpose(0, 2, 1, 3)
    ones_q = jnp.ones((B, FOX_HEADS, 3, T), BF16)
    qa_fox = jnp.concatenate([fq_t.reshape(B, FOX_HEADS, hd, T), c3h, ones_q,
                              jnp.zeros((B, FOX_HEADS, AUG - hd - 6, T), BF16)], axis=2)
    ka_fox = jnp.concatenate([key_major(fk_t, FOX_HEADS), ones_q.transpose(0, 1, 3, 2),
                              -c3h.transpose(0, 1, 3, 2),
                              jnp.zeros((B, FOX_HEADS, T, AUG - hd - 6), BF16)], axis=3)
    one_gate = jnp.ones((B, FOX_HEADS, 1, T), F32)
    o_fox_t = _flash(qa_fox, ka_fox, val_tiles(fv_t, FOX_HEADS, tk), one_gate, tq=256)

    n_chunk = T // CMP_STRIDE
    xc = (kcvc_t.reshape(B, 2 * G, hd, n_chunk, CMP_STRIDE).transpose(0, 1, 3, 4, 2)
          .reshape(B, 2 * G, n_chunk, CMP_STRIDE * hd))
    hsz = CMP_STRIDE * hd
    w1cat = jnp.stack([jnp.concatenate([w[:hsz], w[hsz:]], axis=1) for w in (w_ck1, w_cv1)]).astype(BF16)
    pe8 = jnp.stack([jnp.broadcast_to(p.reshape(1, -1), (8, CMP_LEN * hd)) for p in (pe_k, pe_v)])
    kc, vct = _compress(xc, w1cat, pe8, w_ck2.astype(BF16), w_cv2.T.astype(BF16))

    gates = gt_t.reshape(B, N_BRANCH, G, NSA_REP, T)
    n_sel = T // SEL_BLOCK
    o_cmp_t, selbias = _cmp_attn(nq_t, kc, vct, gates[:, 0:1], n_sel)

    qa_nsa = jnp.concatenate([nqr_t.reshape(B, H, hd, T),
                              jnp.repeat(selbias, NSA_REP, axis=1),
                              jnp.zeros((B, H, AUG - hd - n_sel, T), BF16)], axis=2)
    onehot = (jnp.arange(T)[:, None] // SEL_BLOCK == jnp.arange(n_sel)[None, :]).astype(BF16)
    ka_slc = jnp.concatenate([key_major(ks_t, G), jnp.broadcast_to(onehot, (B, G, T, n_sel)),
                              jnp.zeros((B, G, T, AUG - hd - n_sel), BF16)], axis=3)
    gate_h = gt_t.reshape(B, N_BRANCH, H, 1, T)
    o_slc_t = _flash(qa_nsa, ka_slc, val_tiles(vs_t, G, tk), gate_h[:, 1], tq=256)

    tkw = 128
    ka_win = jnp.concatenate([key_major(kw_t, G), jnp.zeros((B, G, T, AUG - hd), BF16)], axis=3)
    o_win_t = _flash(qa_nsa, ka_win, val_tiles(vw_t, G, tkw), gate_h[:, 2], tq=256, window=WINDOW)

    def token_major(o_t):
        return o_t.reshape(B, -1, T).transpose(0, 2, 1)

    h = _mix_out(h, token_major(o_fox_t), token_major(o_cmp_t), token_major(o_slc_t), token_major(o_win_t),
                 w_mix_out.astype(BF16), row(g_mix_post))

    k_mem, v_mem = _mem_kv(mem, row(g_mem), w_xkv.astype(BF16))
    h = _cross(h, k_mem, v_mem, w_xq.astype(BF16), w_xo.astype(BF16), row(g_x_pre), row(g_x_post))

    h = _mlp(h.reshape(B * T, D), w_up.astype(BF16), w_down.astype(BF16),
             row(g_mlp_pre), row(g_mlp_post)).reshape(B, T, D)
    return h


def kernel(x, mem, g_mix_pre, w_in, b_forget, w_ck1, w_ck2, w_cv1, w_cv2, pe_k, pe_v, w_mix_out, g_mix_post,
           g_x_pre, g_mem, w_xq, w_xkv, w_xo, g_x_post, g_mlp_pre, w_up, w_down, g_mlp_post):
    h = x
    for l in range(g_mix_pre.shape[0]):
        h = _layer(h, mem, g_mix_pre[l], w_in[l], b_forget[l], w_ck1[l], w_ck2[l], w_cv1[l], w_cv2[l],
                   pe_k[l], pe_v[l], w_mix_out[l], g_mix_post[l], g_x_pre[l], g_mem[l], w_xq[l], w_xkv[l],
                   w_xo[l], g_x_post[l], g_mlp_pre[l], w_up[l], w_down[l], g_mlp_post[l])
    return h
```

```python
import functools
import math

import jax
import jax.numpy as jnp
from jax import lax
from jax.experimental import pallas as pl
from jax.experimental.pallas import tpu as pltpu

D_MODEL = 1024
HEAD_DIM = 64
FOX_HEADS = 8
NSA_HEADS = 8
NSA_KV_HEADS = 2
NSA_REP = NSA_HEADS // NSA_KV_HEADS
CMP_LEN = 32
CMP_STRIDE = 16
CMP_HIDDEN = 2 * HEAD_DIM
SEL_BLOCK = 64
SEL_TOPK = 16
WINDOW = 512
N_BRANCH = 3
CROSS_HEADS = 4
CROSS_HEAD_DIM = D_MODEL // CROSS_HEADS
MLP_HIDDEN = 4 * D_MODEL
ROPE_THETA = 10000.0
RMS_EPS = 1e-6
FORCE_SCORE = 1e4
MASKED = -1e30
LOG2E = math.log2(math.e)

FOX_QKV = FOX_HEADS * HEAD_DIM
NSA_Q = NSA_HEADS * HEAD_DIM
NSA_KV = NSA_KV_HEADS * HEAD_DIM
AUG = 128

V7X_VMEM_LIMIT = 56 * 1024 * 1024

F32 = jnp.float32
BF16 = jnp.bfloat16


def _params(sem, vmem=V7X_VMEM_LIMIT):
    return pltpu.CompilerParams(dimension_semantics=sem, vmem_limit_bytes=vmem)


def _rms(x, g):
    return x * lax.rsqrt(jnp.mean(x * x, axis=-1, keepdims=True) + RMS_EPS) * g


def _dot(a, b):
    return jnp.dot(a, b, preferred_element_type=F32)


def _dot_nt(a, b):
    return lax.dot_general(a, b, (((1,), (1,)), ((), ())), preferred_element_type=F32)


def _split3(x):
    hi = x.astype(BF16)
    r1 = x - hi.astype(F32)
    mid = r1.astype(BF16)
    lo = (r1 - mid.astype(F32)).astype(BF16)
    return hi, mid, lo


_R_FQ, _R_FK, _R_FV, _R_NQ = 0, 512, 1024, 1536
_R_KC, _R_VC, _R_KS, _R_VS, _R_KW, _R_VW = 2048, 2176, 2304, 2432, 2560, 2688
_R_FF, _R_NG, _R_END = 2816, 2824, 2848


def _in_proj_kernel(x_ref, g_ref, wt_ref, bf_ref, cos_ref, sin_ref,
                    fq_ref, fk_ref, fv_ref, nq_ref, nqr_ref, kcvc_ref,
                    ks_ref, vs_ref, kw_ref, vw_ref, lf_ref, gt_ref):
    n = _rms(x_ref[0], g_ref[...]).astype(BF16)
    cos = cos_ref[...]
    sin = sin_ref[...]
    half = HEAD_DIM // 2

    def proj(lo, hi):
        return _dot_nt(wt_ref[lo:hi, :], n)

    def store_rope(r, out_ref, heads):
        for h in range(heads):
            x1 = r[h * HEAD_DIM:h * HEAD_DIM + half]
            x2 = r[h * HEAD_DIM + half:(h + 1) * HEAD_DIM]
            out_ref[0, h * HEAD_DIM:h * HEAD_DIM + half, :] = (x1 * cos - x2 * sin).astype(BF16)
            out_ref[0, h * HEAD_DIM + half:(h + 1) * HEAD_DIM, :] = (x2 * cos + x1 * sin).astype(BF16)

    scale = HEAD_DIM ** -0.5 * LOG2E
    fq_ref[0] = (proj(_R_FQ, _R_FK) * scale).astype(BF16)
    fk_ref[0] = proj(_R_FK, _R_FV).astype(BF16)
    fv_ref[0] = proj(_R_FV, _R_NQ).astype(BF16)
    nq = proj(_R_NQ, _R_KC) * scale
    nq_ref[0] = nq.astype(BF16)
    store_rope(nq, nqr_ref, NSA_HEADS)
    kcvc_ref[0] = proj(_R_KC, _R_KS).astype(BF16)
    store_rope(proj(_R_KS, _R_VS), ks_ref, NSA_KV_HEADS)
    vs_ref[0] = proj(_R_VS, _R_KW).astype(BF16)
    store_rope(proj(_R_KW, _R_VW), kw_ref, NSA_KV_HEADS)
    vw_ref[0] = proj(_R_VW, _R_FF).astype(BF16)
    small = proj(_R_FF, _R_END)
    z = small[0:FOX_HEADS] + bf_ref[...]
    lf_ref[0] = jnp.minimum(z, 0.0) - jnp.log1p(jnp.exp(-jnp.abs(z)))
    gt_ref[0] = jax.nn.sigmoid(small[FOX_HEADS:])


def _in_proj(x, g, wt, b_forget, cos_t, sin_t, tm=512):
    B, T, D = x.shape
    grid = (B, T // tm)
    fm = lambda c: pl.BlockSpec((1, c, tm), lambda b, i: (b, 0, i))
    out_shape = [jax.ShapeDtypeStruct((B, c, T), dt) for c, dt in
                 [(512, BF16)] * 5 + [(256, BF16)] + [(128, BF16)] * 4 + [(8, F32), (24, F32)]]
    out_specs = [fm(s.shape[1]) for s in out_shape]
    return pl.pallas_call(
        _in_proj_kernel,
        grid=grid,
        in_specs=[
            pl.BlockSpec((1, tm, D), lambda b, i: (b, i, 0)),
            pl.BlockSpec((1, D), lambda b, i: (0, 0)),
            pl.BlockSpec((_R_END, D), lambda b, i: (0, 0)),
            pl.BlockSpec((FOX_HEADS, 1), lambda b, i: (0, 0)),
            pl.BlockSpec((HEAD_DIM // 2, tm), lambda b, i: (0, i)),
            pl.BlockSpec((HEAD_DIM // 2, tm), lambda b, i: (0, i)),
        ],
        out_specs=out_specs,
        out_shape=out_shape,
        compiler_params=_params(("parallel", "parallel")),
        name="in_proj",
    )(x, g, wt, b_forget, cos_t, sin_t)


_CS = 128


def _cumsum_kernel(x_ref, o_ref):
    x = x_ref[0]
    r = lax.broadcasted_iota(jnp.int32, (_CS, _CS), 0)
    c = lax.broadcasted_iota(jnp.int32, (_CS, _CS), 1)
    tri = (r <= c).astype(BF16)
    xh, xm, xl = _split3(x)
    y = _dot(xh, tri) + _dot(xm, tri) + _dot(xl, tri)
    sh = (_CS // FOX_HEADS).bit_length() - 1
    prev = ((jnp.right_shift(r, sh) == jnp.right_shift(c, sh)) & (c < r)).astype(BF16)
    tot = jnp.broadcast_to(y[:, _CS - 1:_CS], (_CS, _CS))
    th, tm_, tl = _split3(tot)
    carry = _dot(prev, th) + _dot(prev, tm_) + _dot(prev, tl)
    ch, cm, cl = _split3((y + carry) * LOG2E)
    o_ref[0, 0] = ch
    o_ref[0, 1] = cm
    o_ref[0, 2] = cl


def _decay_cumsum(lf):
    B, H, T = lf.shape
    assert H * (T // _CS) == _CS
    x = lf.reshape(B, _CS, _CS)
    out = pl.pallas_call(
        _cumsum_kernel,
        grid=(B,),
        in_specs=[pl.BlockSpec((1, _CS, _CS), lambda b: (b, 0, 0))],
        out_specs=pl.BlockSpec((1, 3, _CS, _CS), lambda b: (b, 0, 0, 0)),
        out_shape=jax.ShapeDtypeStruct((B, 3, _CS, _CS), BF16),
        compiler_params=_params(("parallel",)),
        name="decay_cumsum",
    )(x)
    return out.reshape(B, 3, H, T)


def _flash_kernel(q_ref, k_ref, v_ref, mask_ref, g_ref, o_ref, s_ref, p_ref, *, tq, tk, window):
    i = pl.program_id(2)
    hb = q_ref.shape[1]
    shared = k_ref.shape[1] == 1
    hd = v_ref.shape[3]
    j_hi = ((i + 1) * tq) // tk
    if window is None:
        j_lo = 0
        first_masked = j_hi - tq // tk
        mask_index = lambda j: jnp.maximum(j - first_masked + 1, 0)
    else:
        base = (i * tq - window) // tk
        j_lo = jnp.maximum(base, 0)
        mask_index = lambda j: j - base
    qs = [q_ref[0, h] for h in range(hb)]

    def scores(h, j):
        return _dot(k_ref[0, 0 if shared else h, pl.ds(pl.multiple_of(j * tk, tk), tk), :], qs[h])

    def weighted_values(h, j):
        return _dot(v_ref[0, 0 if shared else h, j], p_ref[h])

    for h in range(hb):
        s_ref[h] = scores(h, j_lo)
        p_ref[h] = jnp.zeros((tk, tq), BF16)

    def body(j, carry):
        mask = mask_ref[mask_index(j)]
        j_next = jnp.minimum(j + 1, j_hi - 1)
        j_prev = jnp.maximum(j - 1, j_lo)
        out = []
        for h in range(hb):
            m, l, acc, alpha = carry[h]
            acc = acc * alpha + weighted_values(h, j_prev)
            s = s_ref[h] + mask
            m_new = jnp.maximum(m, jnp.max(s, axis=0, keepdims=True))
            alpha = jnp.exp2(m - m_new)
            p = jnp.exp2(s - m_new)
            l = alpha * l + jnp.sum(p, axis=0, keepdims=True)
            p_ref[h] = p.astype(BF16)
            s_ref[h] = scores(h, j_next)
            out.append((m_new, l, acc, alpha))
        return tuple(out)

    carry = tuple((jnp.full((1, tq), MASKED, F32), jnp.zeros((1, tq), F32), jnp.zeros((hd, tq), F32),
                   jnp.ones((1, tq), F32)) for _ in range(hb))
    carry = lax.fori_loop(j_lo, j_hi, body, carry)
    for h in range(hb):
        m, l, acc, alpha = carry[h]
        acc = acc * alpha + weighted_values(h, j_hi - 1)
        o_ref[0, h] = (acc * (g_ref[0, h] / l)).astype(o_ref.dtype)


def _mask_table(tq, tk, window):
    k = jnp.arange(tk)[:, None]
    t = jnp.arange(tq)[None, :]
    if window is None:
        tiles = [jnp.ones((tk, tq), bool)] + [(k + d * tk) <= t for d in range(tq // tk)]
    else:
        tiles = [((k + d * tk - window) <= t) & ((k + d * tk - window) > t - window)
                 for d in range((window + tq) // tk)]
    return jnp.where(jnp.stack(tiles), 0.0, MASKED).astype(F32)


def _flash(qa_t, ka, v_tc, gate, *, tq, hb, window=None):
    B, H, _, T = qa_t.shape
    G, nk, hd, tk = v_tc.shape[1:]
    kb = hb if G == H else 1
    assert G == H or hb == H // G
    assert tq % tk == 0 and (window is None or window % tk == 0)
    masks = _mask_table(tq, tk, window)
    return pl.pallas_call(
        functools.partial(_flash_kernel, tq=tq, tk=tk, window=window),
        grid=(B, H // hb, T // tq),
        in_specs=[
            pl.BlockSpec((1, hb, AUG, tq), lambda b, h, i: (b, h, 0, i)),
            pl.BlockSpec((1, kb, T, AUG), lambda b, h, i: (b, h, 0, 0)),
            pl.BlockSpec((1, kb, nk, hd, tk), lambda b, h, i: (b, h, 0, 0, 0)),
            pl.BlockSpec(masks.shape, lambda b, h, i: (0, 0, 0)),
            pl.BlockSpec((1, hb, 1, tq), lambda b, h, i: (b, h, 0, i)),
        ],
        out_specs=pl.BlockSpec((1, hb, hd, tq), lambda b, h, i: (b, h, 0, i)),
        out_shape=jax.ShapeDtypeStruct((B, H, hd, T), BF16),
        scratch_shapes=[pltpu.VMEM((hb, tk, tq), F32), pltpu.VMEM((hb, tk, tq), BF16)],
        compiler_params=_params(("parallel", "parallel", "parallel")),
        name="flash_window" if window is not None else "flash_causal",
    )(qa_t, ka, v_tc, masks, gate)


def _compress_kernel(x_ref, w1_ref, pe_ref, w2k_ref, w2vt_ref, kc_ref, vct_ref):
    half = CMP_STRIDE * HEAD_DIM
    n_chunk = x_ref.shape[2]
    for kind in range(2):
        w1 = w1_ref[kind]
        pe = pe_ref[kind].astype(BF16)
        bias = (_dot(pe[:, :half], w1[:, :CMP_HIDDEN]) + _dot(pe[:, half:], w1[:, CMP_HIDDEN:]))[0:1]
        for g in range(NSA_KV_HEADS):
            ab = _dot(x_ref[0, kind * NSA_KV_HEADS + g], w1)
            nxt = pltpu.roll(ab[:, CMP_HIDDEN:], n_chunk - 1, 0)
            pre = ab[:, :CMP_HIDDEN] + nxt + bias
            hdn = (pre * jax.nn.sigmoid(pre)).astype(BF16)
            if kind == 0:
                kc_ref[0, g] = _dot(hdn, w2k_ref[...]).astype(BF16)
            else:
                vct_ref[0, g] = _dot_nt(w2vt_ref[...], hdn).astype(BF16)


def _compress(xc, w1cat, pe8, w2k, w2vt):
    B = xc.shape[0]
    n_chunk = xc.shape[2]
    full = lambda a: pl.BlockSpec(a.shape, lambda b: (0,) * a.ndim)
    return pl.pallas_call(
        _compress_kernel,
        grid=(B,),
        in_specs=[pl.BlockSpec((1,) + xc.shape[1:], lambda b: (b, 0, 0, 0)),
                  full(w1cat), full(pe8), full(w2k), full(w2vt)],
        out_specs=[pl.BlockSpec((1, NSA_KV_HEADS, n_chunk, HEAD_DIM), lambda b: (b, 0, 0, 0)),
                   pl.BlockSpec((1, NSA_KV_HEADS, HEAD_DIM, n_chunk), lambda b: (b, 0, 0, 0))],
        out_shape=[jax.ShapeDtypeStruct((B, NSA_KV_HEADS, n_chunk, HEAD_DIM), BF16),
                   jax.ShapeDtypeStruct((B, NSA_KV_HEADS, HEAD_DIM, n_chunk), BF16)],
        compiler_params=_params(("parallel",)),
        name="nsa_compress",
    )(xc, w1cat, pe8, w2k, w2vt)


def _cmp_attn_kernel(q_ref, kc_ref, vct_ref, g_ref, o_ref, sb_ref, *, tq):
    i = pl.program_id(2)
    n_cmp = kc_ref.shape[2]
    n_sel = sb_ref.shape[2]
    n_io = lax.broadcasted_iota(jnp.int32, (n_cmp, tq), 0)
    t_io = i * tq + lax.broadcasted_iota(jnp.int32, (n_cmp, tq), 1)
    valid = n_io * CMP_STRIDE + (CMP_LEN - 1) <= t_io
    kc = kc_ref[0, 0]
    vct = vct_ref[0, 0]
    psum = jnp.zeros((n_cmp, tq), F32)
    for r in range(NSA_REP):
        q = q_ref[0, r * HEAD_DIM:(r + 1) * HEAD_DIM, :]
        s = jnp.where(valid, _dot(kc, q), MASKED)
        m = jnp.max(s, axis=0, keepdims=True)
        e = jnp.where(valid, jnp.exp2(s - m), 0.0)
        l = jnp.sum(e, axis=0, keepdims=True)
        p = e * jnp.where(l > 0.0, 1.0 / l, 0.0)
        psum = psum + p
        o = _dot(vct, p.astype(BF16))
        o_ref[0, r * HEAD_DIM:(r + 1) * HEAD_DIM, :] = (o * g_ref[0, 0, 0, r:r + 1, :]).astype(BF16)

    jr = lax.broadcasted_iota(jnp.int32, (n_sel, n_cmp), 0)
    nc = lax.broadcasted_iota(jnp.int32, (n_sel, n_cmp), 1)
    overlap = ((nc * CMP_STRIDE < (jr + 1) * SEL_BLOCK)
               & (nc * CMP_STRIDE + CMP_LEN > jr * SEL_BLOCK)).astype(BF16)
    ph, pm, pl_ = _split3(psum)
    imp = _dot(overlap, ph) + _dot(overlap, pm) + _dot(overlap, pl_)
    j_io = lax.broadcasted_iota(jnp.int32, (n_sel, tq), 0)
    cur = jnp.right_shift(i * tq + lax.broadcasted_iota(jnp.int32, (n_sel, tq), 1),
                          SEL_BLOCK.bit_length() - 1)
    is_cur = j_io == cur
    is_fixed = (j_io == 0) | (j_io == cur - 1)
    imp = jnp.where(is_cur, 2.0 * FORCE_SCORE, jnp.where(is_fixed, FORCE_SCORE, imp))
    imp = jnp.where(j_io <= cur, imp, -1.0)
    cnt = jnp.zeros((n_sel, tq), jnp.int32)
    for jp in range(n_sel):
        row = imp[jp:jp + 1, :]
        beats = (row > imp) | ((row == imp) & (j_io > jp))
        cnt = cnt + beats.astype(jnp.int32)
    sb_ref[0, 0] = jnp.where(cnt < min(SEL_TOPK, n_sel), 0.0, MASKED).astype(BF16)


def _cmp_attn(nq_t, kc, vct, gate_cmp, n_sel, tq=256):
    B, _, T = nq_t.shape
    G = kc.shape[1]
    rows = NSA_REP * HEAD_DIM
    return pl.pallas_call(
        functools.partial(_cmp_attn_kernel, tq=tq),
        grid=(B, G, T // tq),
        in_specs=[
            pl.BlockSpec((1, rows, tq), lambda b, g, i: (b, g, i)),
            pl.BlockSpec((1, 1) + kc.shape[2:], lambda b, g, i: (b, g, 0, 0)),
            pl.BlockSpec((1, 1) + vct.shape[2:], lambda b, g, i: (b, g, 0, 0)),
            pl.BlockSpec((1, 1, 1, NSA_REP, tq), lambda b, g, i: (b, 0, g, 0, i)),
        ],
        out_specs=[pl.BlockSpec((1, rows, tq), lambda b, g, i: (b, g, i)),
                   pl.BlockSpec((1, 1, n_sel, tq), lambda b, g, i: (b, g, 0, i))],
        out_shape=[jax.ShapeDtypeStruct((B, G * rows, T), BF16),
                   jax.ShapeDtypeStruct((B, G, n_sel, T), BF16)],
        compiler_params=_params(("parallel", "parallel", "parallel")),
        name="nsa_cmp_attn",
    )(nq_t, kc, vct, gate_cmp)


def _mix_out_kernel(x_ref, of_ref, oc_ref, os_ref, ow_ref, wo_ref, g_ref, o_ref):
    nsa = (oc_ref[0].astype(F32) + os_ref[0].astype(F32) + ow_ref[0].astype(F32)).astype(BF16)
    mix = _dot(of_ref[0], wo_ref[0:FOX_QKV, :]) + _dot(nsa, wo_ref[FOX_QKV:, :])
    o_ref[0] = x_ref[0] + _rms(mix, g_ref[...])


def _mix_out(x, o_fox, o_cmp, o_slc, o_win, wo, g, tm=512):
    B, T, D = x.shape
    row = lambda c: pl.BlockSpec((1, tm, c), lambda b, i: (b, i, 0))
    return pl.pallas_call(
        _mix_out_kernel,
        grid=(B, T // tm),
        in_specs=[row(D), row(FOX_QKV), row(NSA_Q), row(NSA_Q), row(NSA_Q),
                  pl.BlockSpec(wo.shape, lambda b, i: (0, 0)),
                  pl.BlockSpec((1, D), lambda b, i: (0, 0))],
        out_specs=row(D),
        out_shape=jax.ShapeDtypeStruct((B, T, D), F32),
        compiler_params=_params(("parallel", "parallel")),
        name="mix_out",
    )(x, o_fox, o_cmp, o_slc, o_win, wo, g)


def _mem_kv_kernel(m_ref, g_ref, w_ref, k_ref, v_ref):
    m = _rms(m_ref[0], g_ref[...]).astype(BF16)
    k_ref[0] = _dot(m, w_ref[:, :D_MODEL]).astype(BF16)
    v_ref[0] = _dot(m, w_ref[:, D_MODEL:]).astype(BF16)


def _mem_kv(mem, g, wkv):
    B, M, D = mem.shape
    blk = pl.BlockSpec((1, M, D), lambda b: (b, 0, 0))
    return pl.pallas_call(
        _mem_kv_kernel,
        grid=(B,),
        in_specs=[blk, pl.BlockSpec((1, D), lambda b: (0, 0)), pl.BlockSpec(wkv.shape, lambda b: (0, 0))],
        out_specs=[blk, blk],
        out_shape=[jax.ShapeDtypeStruct((B, M, D), BF16)] * 2,
        compiler_params=_params(("parallel",)),
        name="mem_kv",
    )(mem, g, wkv)


def _cross_kernel(h_ref, k_ref, v_ref, wq_ref, wo_ref, gpre_ref, gpost_ref, o_ref):
    h = h_ref[0]
    n = _rms(h, gpre_ref[...]).astype(BF16)
    scale = CROSS_HEAD_DIM ** -0.5
    acc = jnp.zeros(h.shape, F32)
    for hh in range(CROSS_HEADS):
        sl = slice(hh * CROSS_HEAD_DIM, (hh + 1) * CROSS_HEAD_DIM)
        q = (_dot(n, wq_ref[:, sl]) * scale).astype(BF16)
        s = _dot_nt(q, k_ref[0, :, sl])
        m = jnp.max(s, axis=-1, keepdims=True)
        p = jnp.exp(s - m)
        l = jnp.sum(p, axis=-1, keepdims=True)
        o = _dot(p.astype(BF16), v_ref[0, :, sl]) / l
        acc = acc + _dot(o.astype(BF16), wo_ref[sl, :])
    o_ref[0] = h + _rms(acc, gpost_ref[...])


def _cross(h, k, v, wq, wo, g_pre, g_post, tm=256):
    B, T, D = h.shape
    M = k.shape[1]
    row = pl.BlockSpec((1, tm, D), lambda b, i: (b, i, 0))
    kvb = pl.BlockSpec((1, M, D), lambda b, i: (b, 0, 0))
    wsp = pl.BlockSpec((D, D), lambda b, i: (0, 0))
    gsp = pl.BlockSpec((1, D), lambda b, i: (0, 0))
    return pl.pallas_call(
        _cross_kernel,
        grid=(B, T // tm),
        in_specs=[row, kvb, kvb, wsp, wsp, gsp, gsp],
        out_specs=row,
        out_shape=jax.ShapeDtypeStruct((B, T, D), F32),
        compiler_params=_params(("parallel", "parallel")),
        name="mem_cross",
    )(h, k, v, wq, wo, g_pre, g_post)


def _mlp_kernel(h_ref, wu_ref, wd_ref, gpre_ref, gpost_ref, o_ref, *, hc):
    h = h_ref[...]
    n = _rms(h, gpre_ref[...]).astype(BF16)
    acc = jnp.zeros(h.shape, F32)
    for c in range(wu_ref.shape[1] // hc):
        u = jnp.maximum(_dot(n, wu_ref[:, c * hc:(c + 1) * hc]), 0.0)
        acc = acc + _dot((u * u).astype(BF16), wd_ref[c * hc:(c + 1) * hc, :])
    o_ref[...] = h + _rms(acc, gpost_ref[...])


def _mlp(h, wu, wd, g_pre, g_post, tm=512, hc=512):
    N, D = h.shape
    row = pl.BlockSpec((tm, D), lambda i: (i, 0))
    gsp = pl.BlockSpec((1, D), lambda i: (0, 0))
    once = pl.Buffered(1)
    return pl.pallas_call(
        functools.partial(_mlp_kernel, hc=hc),
        grid=(N // tm,),
        in_specs=[row,
                  pl.BlockSpec(wu.shape, lambda i: (0, 0), pipeline_mode=once),
                  pl.BlockSpec(wd.shape, lambda i: (0, 0), pipeline_mode=once),
                  gsp, gsp],
        out_specs=row,
        out_shape=jax.ShapeDtypeStruct((N, D), F32),
        compiler_params=_params(("parallel",)),
        name="relu2_mlp",
    )(h, wu, wd, g_pre, g_post)


def _layer(h, mem, g_mix_pre, w_in, b_forget, w_ck1, w_ck2, w_cv1, w_cv2, pe_k, pe_v,
           w_mix_out, g_mix_post, g_x_pre, g_mem, w_xq, w_xkv, w_xo, g_x_post,
           g_mlp_pre, w_up, w_down, g_mlp_post):
    B, T, D = h.shape
    H, G, hd = NSA_HEADS, NSA_KV_HEADS, HEAD_DIM
    row = lambda g: g.reshape(1, -1)

    cols = {}
    lo = 0
    for name, size in (("fq", FOX_QKV), ("fk", FOX_QKV), ("fv", FOX_QKV), ("ff", FOX_HEADS), ("nq", NSA_Q),
                       ("kc", NSA_KV), ("vc", NSA_KV), ("ks", NSA_KV), ("vs", NSA_KV), ("kw", NSA_KV),
                       ("vw", NSA_KV), ("ng", NSA_HEADS * N_BRANCH)):
        cols[name] = w_in[:, lo:lo + size]
        lo += size
    ng_branch_major = cols["ng"].reshape(D, NSA_HEADS, N_BRANCH).transpose(0, 2, 1).reshape(D, -1)
    wt = jnp.concatenate([cols[k] for k in ("fq", "fk", "fv", "nq", "kc", "vc", "ks", "vs", "kw", "vw", "ff")]
                         + [ng_branch_major], axis=1).T.astype(BF16)
    half = hd // 2
    inv = ROPE_THETA ** (-jnp.arange(half, dtype=F32) / half)
    ang = inv[:, None] * jnp.arange(T, dtype=F32)[None, :]
    cos_t, sin_t = jnp.cos(ang), jnp.sin(ang)

    (fq_t, fk_t, fv_t, nq_t, nqr_t, kcvc_t, ks_t, vs_t, kw_t, vw_t, lf_t, gt_t) = _in_proj(
        h, row(g_mix_pre), wt, b_forget.reshape(FOX_HEADS, 1), cos_t, sin_t)

    tk = 256

    def key_major(a_t, heads):
        return a_t.reshape(B, heads, hd, T).transpose(0, 1, 3, 2)

    def val_tiles(a_t, heads, tk_):
        return a_t.reshape(B, heads, hd, T // tk_, tk_).transpose(0, 1, 3, 2, 4)

    c3 = _decay_cumsum(lf_t)
    c3h = c3.transpose(0, 2, 1, 3)
    ones_q = jnp.ones((B, FOX_HEADS, 3, T), BF16)
    qa_fox = jnp.concatenate([fq_t.reshape(B, FOX_HEADS, hd, T), c3h, ones_q,
                              jnp.zeros((B, FOX_HEADS, AUG - hd - 6, T), BF16)], axis=2)
    ka_fox = jnp.concatenate([key_major(fk_t, FOX_HEADS), ones_q.transpose(0, 1, 3, 2),
                              -c3h.transpose(0, 1, 3, 2),
                              jnp.zeros((B, FOX_HEADS, T, AUG - hd - 6), BF16)], axis=3)
    one_gate = jnp.ones((B, FOX_HEADS, 1, T), F32)
    o_fox_t = _flash(qa_fox, ka_fox, val_tiles(fv_t, FOX_HEADS, tk), one_gate, tq=256, hb=4)

    n_chunk = T // CMP_STRIDE
    xc = (kcvc_t.reshape(B, 2 * G, hd, n_chunk, CMP_STRIDE).transpose(0, 1, 3, 4, 2)
          .reshape(B, 2 * G, n_chunk, CMP_STRIDE * hd))
    hsz = CMP_STRIDE * hd
    w1cat = jnp.stack([jnp.concatenate([w[:hsz], w[hsz:]], axis=1) for w in (w_ck1, w_cv1)]).astype(BF16)
    pe8 = jnp.stack([jnp.broadcast_to(p.reshape(1, -1), (8, CMP_LEN * hd)) for p in (pe_k, pe_v)])
    kc, vct = _compress(xc, w1cat, pe8, w_ck2.astype(BF16), w_cv2.T.astype(BF16))

    gates = gt_t.reshape(B, N_BRANCH, G, NSA_REP, T)
    n_sel = T // SEL_BLOCK
    o_cmp_t, selbias = _cmp_attn(nq_t, kc, vct, gates[:, 0:1], n_sel)

    qa_nsa = jnp.concatenate([nqr_t.reshape(B, H, hd, T),
                              jnp.repeat(selbias, NSA_REP, axis=1),
                              jnp.zeros((B, H, AUG - hd - n_sel, T), BF16)], axis=2)
    onehot = (jnp.arange(T)[:, None] // SEL_BLOCK == jnp.arange(n_sel)[None, :]).astype(BF16)
    ka_slc = jnp.concatenate([key_major(ks_t, G), jnp.broadcast_to(onehot, (B, G, T, n_sel)),
                              jnp.zeros((B, G, T, AUG - hd - n_sel), BF16)], axis=3)
    gate_h = gt_t.reshape(B, N_BRANCH, H, 1, T)
    o_slc_t = _flash(qa_nsa, ka_slc, val_tiles(vs_t, G, tk), gate_h[:, 1], tq=256, hb=NSA_REP)

    ka_win = jnp.concatenate([key_major(kw_t, G), jnp.zeros((B, G, T, AUG - hd), BF16)], axis=3)
    o_win_t = _flash(qa_nsa, ka_win, val_tiles(vw_t, G, tk), gate_h[:, 2], tq=256, hb=NSA_REP, window=WINDOW)

    def token_major(o_t):
        return o_t.reshape(B, -1, T).transpose(0, 2, 1)

    h = _mix_out(h, token_major(o_fox_t), token_major(o_cmp_t), token_major(o_slc_t), token_major(o_win_t),
                 w_mix_out.astype(BF16), row(g_mix_post))

    k_mem, v_mem = _mem_kv(mem, row(g_mem), w_xkv.astype(BF16))
    h = _cross(h, k_mem, v_mem, w_xq.astype(BF16), w_xo.astype(BF16), row(g_x_pre), row(g_x_post))

    h = _mlp(h.reshape(B * T, D), w_up.astype(BF16), w_down.astype(BF16),
             row(g_mlp_pre), row(g_mlp_post)).reshape(B, T, D)
    return h


def kernel(x, mem, g_mix_pre, w_in, b_forget, w_ck1, w_ck2, w_cv1, w_cv2, pe_k, pe_v, w_mix_out, g_mix_post,
           g_x_pre, g_mem, w_xq, w_xkv, w_xo, g_x_post, g_mlp_pre, w_up, w_down, g_mlp_post):
    h = x
    for l in range(g_mix_pre.shape[0]):
        h = _layer(h, mem, g_mix_pre[l], w_in[l], b_forget[l], w_ck1[l], w_ck2[l], w_cv1[l], w_cv2[l],
                   pe_k[l], pe_v[l], w_mix_out[l], g_mix_post[l], g_x_pre[l], g_mem[l], w_xq[l], w_xkv[l],
                   w_xo[l], g_x_post[l], g_mlp_pre[l], w_up[l], w_down[l], g_mlp_post[l])
    return h
```

```python
import functools
import math

import jax
import jax.numpy as jnp
from jax import lax
from jax.experimental import pallas as pl
from jax.experimental.pallas import tpu as pltpu

D_MODEL = 1024
HEAD_DIM = 64
FOX_HEADS = 8
NSA_HEADS = 8
NSA_KV_HEADS = 2
NSA_REP = NSA_HEADS // NSA_KV_HEADS
CMP_LEN = 32
CMP_STRIDE = 16
CMP_HIDDEN = 2 * HEAD_DIM
SEL_BLOCK = 64
SEL_TOPK = 16
WINDOW = 512
N_BRANCH = 3
CROSS_HEADS = 4
CROSS_HEAD_DIM = D_MODEL // CROSS_HEADS
MLP_HIDDEN = 4 * D_MODEL
ROPE_THETA = 10000.0
RMS_EPS = 1e-6
FORCE_SCORE = 1e4
MASKED = -1e30
LOG2E = math.log2(math.e)

FOX_QKV = FOX_HEADS * HEAD_DIM
NSA_Q = NSA_HEADS * HEAD_DIM
NSA_KV = NSA_KV_HEADS * HEAD_DIM
AUG = 128

V7X_VMEM_LIMIT = 56 * 1024 * 1024

F32 = jnp.float32
BF16 = jnp.bfloat16


def _params(sem, vmem=V7X_VMEM_LIMIT):
    return pltpu.CompilerParams(dimension_semantics=sem, vmem_limit_bytes=vmem)


def _rms(x, g):
    return x * lax.rsqrt(jnp.mean(x * x, axis=-1, keepdims=True) + RMS_EPS) * g


def _dot(a, b):
    return jnp.dot(a, b, preferred_element_type=F32)


def _dot_nt(a, b):
    return lax.dot_general(a, b, (((1,), (1,)), ((), ())), preferred_element_type=F32)


def _split3(x):
    hi = x.astype(BF16)
    r1 = x - hi.astype(F32)
    mid = r1.astype(BF16)
    lo = (r1 - mid.astype(F32)).astype(BF16)
    return hi, mid, lo


_R_FQ, _R_FK, _R_FV, _R_NQ = 0, 512, 1024, 1536
_R_KC, _R_VC, _R_KS, _R_VS, _R_KW, _R_VW = 2048, 2176, 2304, 2432, 2560, 2688
_R_FF, _R_NG, _R_END = 2816, 2824, 2848


def _in_proj_kernel(x_ref, g_ref, wt_ref, bf_ref, cos_ref, sin_ref,
                    fq_ref, fk_ref, fv_ref, nq_ref, nqr_ref, kcvc_ref,
                    ks_ref, vs_ref, kw_ref, vw_ref, lf_ref, gt_ref):
    n = _rms(x_ref[0], g_ref[...]).astype(BF16)
    cos = cos_ref[...]
    sin = sin_ref[...]
    half = HEAD_DIM // 2

    def proj(lo, hi):
        return _dot_nt(wt_ref[lo:hi, :], n)

    def store_rope(r, out_ref, heads):
        for h in range(heads):
            x1 = r[h * HEAD_DIM:h * HEAD_DIM + half]
            x2 = r[h * HEAD_DIM + half:(h + 1) * HEAD_DIM]
            out_ref[0, h * HEAD_DIM:h * HEAD_DIM + half, :] = (x1 * cos - x2 * sin).astype(BF16)
            out_ref[0, h * HEAD_DIM + half:(h + 1) * HEAD_DIM, :] = (x2 * cos + x1 * sin).astype(BF16)

    scale = HEAD_DIM ** -0.5 * LOG2E
    fq_ref[0] = (proj(_R_FQ, _R_FK) * scale).astype(BF16)
    fk_ref[0] = proj(_R_FK, _R_FV).astype(BF16)
    fv_ref[0] = proj(_R_FV, _R_NQ).astype(BF16)
    nq = proj(_R_NQ, _R_KC) * scale
    nq_ref[0] = nq.astype(BF16)
    store_rope(nq, nqr_ref, NSA_HEADS)
    kcvc_ref[0] = proj(_R_KC, _R_KS).astype(BF16)
    store_rope(proj(_R_KS, _R_VS), ks_ref, NSA_KV_HEADS)
    vs_ref[0] = proj(_R_VS, _R_KW).astype(BF16)
    store_rope(proj(_R_KW, _R_VW), kw_ref, NSA_KV_HEADS)
    vw_ref[0] = proj(_R_VW, _R_FF).astype(BF16)
    small = proj(_R_FF, _R_END)
    z = small[0:FOX_HEADS] + bf_ref[...]
    lf_ref[0] = jnp.minimum(z, 0.0) - jnp.log1p(jnp.exp(-jnp.abs(z)))
    gt_ref[0] = jax.nn.sigmoid(small[FOX_HEADS:])


def _in_proj(x, g, wt, b_forget, cos_t, sin_t, tm=512):
    B, T, D = x.shape
    grid = (B, T // tm)
    fm = lambda c: pl.BlockSpec((1, c, tm), lambda b, i: (b, 0, i))
    out_shape = [jax.ShapeDtypeStruct((B, c, T), dt) for c, dt in
                 [(512, BF16)] * 5 + [(256, BF16)] + [(128, BF16)] * 4 + [(8, F32), (24, F32)]]
    out_specs = [fm(s.shape[1]) for s in out_shape]
    return pl.pallas_call(
        _in_proj_kernel,
        grid=grid,
        in_specs=[
            pl.BlockSpec((1, tm, D), lambda b, i: (b, i, 0)),
            pl.BlockSpec((1, D), lambda b, i: (0, 0)),
            pl.BlockSpec((_R_END, D), lambda b, i: (0, 0)),
            pl.BlockSpec((FOX_HEADS, 1), lambda b, i: (0, 0)),
            pl.BlockSpec((HEAD_DIM // 2, tm), lambda b, i: (0, i)),
            pl.BlockSpec((HEAD_DIM // 2, tm), lambda b, i: (0, i)),
        ],
        out_specs=out_specs,
        out_shape=out_shape,
        compiler_params=_params(("parallel", "parallel")),
        name="in_proj",
    )(x, g, wt, b_forget, cos_t, sin_t)


_CS = 128


def _cumsum_kernel(x_ref, o_ref):
    x = x_ref[0]
    r = lax.broadcasted_iota(jnp.int32, (_CS, _CS), 0)
    c = lax.broadcasted_iota(jnp.int32, (_CS, _CS), 1)
    tri = (r <= c).astype(BF16)
    xh, xm, xl = _split3(x)
    y = _dot(xh, tri) + _dot(xm, tri) + _dot(xl, tri)
    sh = (_CS // FOX_HEADS).bit_length() - 1
    prev = ((jnp.right_shift(r, sh) == jnp.right_shift(c, sh)) & (c < r)).astype(BF16)
    tot = jnp.broadcast_to(y[:, _CS - 1:_CS], (_CS, _CS))
    th, tm_, tl = _split3(tot)
    carry = _dot(prev, th) + _dot(prev, tm_) + _dot(prev, tl)
    ch, cm, cl = _split3((y + carry) * LOG2E)
    o_ref[0, 0] = ch
    o_ref[0, 1] = cm
    o_ref[0, 2] = cl


def _decay_cumsum(lf):
    B, H, T = lf.shape
    assert H * (T // _CS) == _CS
    x = lf.reshape(B, _CS, _CS)
    out = pl.pallas_call(
        _cumsum_kernel,
        grid=(B,),
        in_specs=[pl.BlockSpec((1, _CS, _CS), lambda b: (b, 0, 0))],
        out_specs=pl.BlockSpec((1, 3, _CS, _CS), lambda b: (b, 0, 0, 0)),
        out_shape=jax.ShapeDtypeStruct((B, 3, _CS, _CS), BF16),
        compiler_params=_params(("parallel",)),
        name="decay_cumsum",
    )(x)
    return out.reshape(B, 3, H, T)


def _flash_kernel(q_ref, k_ref, v_ref, mask_ref, g_ref, *rest, t, back, n_extra):
    extras = rest[:n_extra]
    o_ref, s_ref, p_ref = rest[n_extra:]
    hb, nq = q_ref.shape[1], q_ref.shape[2]
    shared = k_ref.shape[1] == 1
    hd = v_ref.shape[3]
    n_pairs = sum(min(i, back) + 1 for i in range(nq))

    def scores(h, i, j):
        return _dot(k_ref[0, 0 if shared else h, pl.ds(pl.multiple_of(j * t, t), t), :], q_ref[0, h, i])

    def weighted_values(h, j):
        return _dot(v_ref[0, 0 if shared else h, j], p_ref[h])

    def emit(h, i, l, acc):
        o_ref[0, h, i] = (acc * (g_ref[0, h, i] / l)).astype(o_ref.dtype)

    for h in range(hb):
        s_ref[h] = scores(h, 0, 0)
        p_ref[h] = jnp.zeros((t, t), BF16)

    def body(_, carry):
        (i, j, i_prev, j_prev), heads = carry
        row_end = j == i
        i_next = jnp.minimum(jnp.where(row_end, i + 1, i), nq - 1)
        j_next = jnp.where(row_end, jnp.maximum(i_next - back, 0), j + 1)
        row_start = j == jnp.maximum(i - back, 0)
        mask = mask_ref[jnp.maximum(j - i + (mask_ref.shape[0] - 1), 0)]
        out = []
        for h in range(hb):
            m, l, acc, alpha = heads[h]
            acc = acc * alpha + weighted_values(h, j_prev)
            emit(h, i_prev, l, acc)
            m = jnp.where(row_start, MASKED, m)
            s = s_ref[h] + mask
            m_new = jnp.maximum(m, jnp.max(s, axis=0, keepdims=True))
            alpha = jnp.exp2(m - m_new)
            p = jnp.exp2(s - m_new)
            l = alpha * l + jnp.sum(p, axis=0, keepdims=True)
            p_ref[h] = p.astype(BF16)
            s_ref[h] = scores(h, i_next, j_next)
            out.append((m_new, l, acc, alpha))
        return (i_next, j_next, i, j), tuple(out)

    zero = jnp.int32(0)
    heads = tuple((jnp.full((1, t), MASKED, F32), jnp.ones((1, t), F32), jnp.zeros((hd, t), F32),
                   jnp.ones((1, t), F32)) for _ in range(hb))
    _, heads = lax.fori_loop(0, n_pairs, body, ((zero, zero, zero, zero), heads))
    for h in range(hb):
        m, l, acc, alpha = heads[h]
        emit(h, nq - 1, l, acc * alpha + weighted_values(h, nq - 1))
    for x_ref in extras:
        for h in range(hb):
            for i in range(nq):
                o_ref[0, h, i] = (o_ref[0, h, i].astype(F32) + x_ref[0, h, i].astype(F32)).astype(o_ref.dtype)


def _mask_table(t, n, window):
    k = jnp.arange(t)[:, None]
    q = jnp.arange(t)[None, :]
    tiles = []
    for d in range(n):
        off = (d - (n - 1)) * t
        ok = (k + off) <= q
        if window is not None:
            ok = ok & ((k + off) > q - window)
        tiles.append(ok)
    return jnp.where(jnp.stack(tiles), 0.0, MASKED).astype(F32)


def _flash(qa, ka, v_t, gate, extras=(), *, hb, window=None):
    B, H, nq, _, t = qa.shape
    G, _, hd, _ = v_t.shape[1:]
    T = nq * t
    kb = hb if G == H else 1
    assert G == H or hb == H // G
    if window is None:
        back = nq - 1
        masks = _mask_table(t, 2, None)
    else:
        assert window % t == 0
        back = window // t
        masks = _mask_table(t, back + 1, window)
    tile = lambda c: pl.BlockSpec((1, hb, nq, c, t), lambda b, h: (b, h, 0, 0, 0))
    return pl.pallas_call(
        functools.partial(_flash_kernel, t=t, back=back, n_extra=len(extras)),
        grid=(B, H // hb),
        in_specs=[
            tile(AUG),
            pl.BlockSpec((1, kb, T, AUG), lambda b, h: (b, h, 0, 0)),
            pl.BlockSpec((1, kb, nq, hd, t), lambda b, h: (b, h, 0, 0, 0)),
            pl.BlockSpec(masks.shape, lambda b, h: (0, 0, 0)),
            tile(1),
        ] + [tile(hd)] * len(extras),
        out_specs=tile(hd),
        out_shape=jax.ShapeDtypeStruct((B, H, nq, hd, t), BF16),
        scratch_shapes=[pltpu.VMEM((hb, t, t), F32), pltpu.VMEM((hb, t, t), BF16)],
        compiler_params=_params(("parallel", "parallel")),
        name="flash_window" if window is not None else "flash_causal",
    )(qa, ka, v_t, masks, gate, *extras)


def _compress_kernel(x_ref, w1_ref, pe_ref, w2k_ref, w2vt_ref, kc_ref, vct_ref):
    half = CMP_STRIDE * HEAD_DIM
    n_chunk = x_ref.shape[2]
    for kind in range(2):
        w1 = w1_ref[kind]
        pe = pe_ref[kind].astype(BF16)
        bias = (_dot(pe[:, :half], w1[:, :CMP_HIDDEN]) + _dot(pe[:, half:], w1[:, CMP_HIDDEN:]))[0:1]
        for g in range(NSA_KV_HEADS):
            ab = _dot(x_ref[0, kind * NSA_KV_HEADS + g], w1)
            nxt = pltpu.roll(ab[:, CMP_HIDDEN:], n_chunk - 1, 0)
            pre = ab[:, :CMP_HIDDEN] + nxt + bias
            hdn = (pre * jax.nn.sigmoid(pre)).astype(BF16)
            if kind == 0:
                kc_ref[0, g] = _dot(hdn, w2k_ref[...]).astype(BF16)
            else:
                vct_ref[0, g] = _dot_nt(w2vt_ref[...], hdn).astype(BF16)


def _compress(xc, w1cat, pe8, w2k, w2vt):
    B = xc.shape[0]
    n_chunk = xc.shape[2]
    full = lambda a: pl.BlockSpec(a.shape, lambda b: (0,) * a.ndim)
    return pl.pallas_call(
        _compress_kernel,
        grid=(B,),
        in_specs=[pl.BlockSpec((1,) + xc.shape[1:], lambda b: (b, 0, 0, 0)),
                  full(w1cat), full(pe8), full(w2k), full(w2vt)],
        out_specs=[pl.BlockSpec((1, NSA_KV_HEADS, n_chunk, HEAD_DIM), lambda b: (b, 0, 0, 0)),
                   pl.BlockSpec((1, NSA_KV_HEADS, HEAD_DIM, n_chunk), lambda b: (b, 0, 0, 0))],
        out_shape=[jax.ShapeDtypeStruct((B, NSA_KV_HEADS, n_chunk, HEAD_DIM), BF16),
                   jax.ShapeDtypeStruct((B, NSA_KV_HEADS, HEAD_DIM, n_chunk), BF16)],
        compiler_params=_params(("parallel",)),
        name="nsa_compress",
    )(xc, w1cat, pe8, w2k, w2vt)


def _cmp_attn_kernel(q_ref, kc_ref, vct_ref, g_ref, o_ref, sb_ref, *, tq, t):
    i = pl.program_id(2)
    n_cmp = kc_ref.shape[2]
    n_sel = sb_ref.shape[2]
    n_io = lax.broadcasted_iota(jnp.int32, (n_cmp, tq), 0)
    t_io = i * tq + lax.broadcasted_iota(jnp.int32, (n_cmp, tq), 1)
    valid = n_io * CMP_STRIDE + (CMP_LEN - 1) <= t_io
    kc = kc_ref[0, 0]
    vct = vct_ref[0, 0]
    psum = jnp.zeros((n_cmp, tq), F32)
    for r in range(NSA_REP):
        q = q_ref[0, r * HEAD_DIM:(r + 1) * HEAD_DIM, :]
        s = jnp.where(valid, _dot(kc, q), MASKED)
        m = jnp.max(s, axis=0, keepdims=True)
        e = jnp.where(valid, jnp.exp2(s - m), 0.0)
        l = jnp.sum(e, axis=0, keepdims=True)
        p = e * jnp.where(l > 0.0, 1.0 / l, 0.0)
        psum = psum + p
        o = _dot(vct, p.astype(BF16))
        o = (o * g_ref[0, 0, 0, r:r + 1, :]).astype(BF16)
        for c in range(tq // t):
            o_ref[0, r, c] = o[:, c * t:(c + 1) * t]

    jr = lax.broadcasted_iota(jnp.int32, (n_sel, n_cmp), 0)
    nc = lax.broadcasted_iota(jnp.int32, (n_sel, n_cmp), 1)
    overlap = ((nc * CMP_STRIDE < (jr + 1) * SEL_BLOCK)
               & (nc * CMP_STRIDE + CMP_LEN > jr * SEL_BLOCK)).astype(BF16)
    ph, pm, pl_ = _split3(psum)
    imp = _dot(overlap, ph) + _dot(overlap, pm) + _dot(overlap, pl_)
    j_io = lax.broadcasted_iota(jnp.int32, (n_sel, tq), 0)
    cur = jnp.right_shift(i * tq + lax.broadcasted_iota(jnp.int32, (n_sel, tq), 1),
                          SEL_BLOCK.bit_length() - 1)
    is_cur = j_io == cur
    is_fixed = (j_io == 0) | (j_io == cur - 1)
    imp = jnp.where(is_cur, 2.0 * FORCE_SCORE, jnp.where(is_fixed, FORCE_SCORE, imp))
    imp = jnp.where(j_io <= cur, imp, -1.0)
    cnt = jnp.zeros((n_sel, tq), jnp.int32)
    for jp in range(n_sel):
        row = imp[jp:jp + 1, :]
        beats = (row > imp) | ((row == imp) & (j_io > jp))
        cnt = cnt + beats.astype(jnp.int32)
    sb_ref[0, 0] = jnp.where(cnt < min(SEL_TOPK, n_sel), 0.0, MASKED).astype(BF16)


def _cmp_attn(nq_t, kc, vct, gate_cmp, n_sel, t, tq=1024):
    B, _, T = nq_t.shape
    G = kc.shape[1]
    rows = NSA_REP * HEAD_DIM
    return pl.pallas_call(
        functools.partial(_cmp_attn_kernel, tq=tq, t=t),
        grid=(B, G, T // tq),
        in_specs=[
            pl.BlockSpec((1, rows, tq), lambda b, g, i: (b, g, i)),
            pl.BlockSpec((1, 1) + kc.shape[2:], lambda b, g, i: (b, g, 0, 0)),
            pl.BlockSpec((1, 1) + vct.shape[2:], lambda b, g, i: (b, g, 0, 0)),
            pl.BlockSpec((1, 1, 1, NSA_REP, tq), lambda b, g, i: (b, 0, g, 0, i)),
        ],
        out_specs=[pl.BlockSpec((1, NSA_REP, tq // t, HEAD_DIM, t), lambda b, g, i: (b, g, i, 0, 0)),
                   pl.BlockSpec((1, 1, n_sel, tq), lambda b, g, i: (b, g, 0, i))],
        out_shape=[jax.ShapeDtypeStruct((B, G * NSA_REP, T // t, HEAD_DIM, t), BF16),
                   jax.ShapeDtypeStruct((B, G, n_sel, T), BF16)],
        compiler_params=_params(("parallel", "parallel", "parallel")),
        name="nsa_cmp_attn",
    )(nq_t, kc, vct, gate_cmp)


def _mix_out_kernel(x_ref, of_ref, on_ref, wo_ref, g_ref, o_ref):
    mix = _dot(of_ref[0], wo_ref[0:FOX_QKV, :]) + _dot(on_ref[0], wo_ref[FOX_QKV:, :])
    o_ref[0] = x_ref[0] + _rms(mix, g_ref[...])


def _mix_out(x, o_fox, o_nsa, wo, g, tm=512):
    B, T, D = x.shape
    row = lambda c: pl.BlockSpec((1, tm, c), lambda b, i: (b, i, 0))
    return pl.pallas_call(
        _mix_out_kernel,
        grid=(B, T // tm),
        in_specs=[row(D), row(FOX_QKV), row(NSA_Q),
                  pl.BlockSpec(wo.shape, lambda b, i: (0, 0)),
                  pl.BlockSpec((1, D), lambda b, i: (0, 0))],
        out_specs=row(D),
        out_shape=jax.ShapeDtypeStruct((B, T, D), F32),
        compiler_params=_params(("parallel", "parallel")),
        name="mix_out",
    )(x, o_fox, o_nsa, wo, g)


def _mem_kv_kernel(m_ref, g_ref, w_ref, k_ref, v_ref):
    m = _rms(m_ref[0], g_ref[...]).astype(BF16)
    k_ref[0] = _dot(m, w_ref[:, :D_MODEL]).astype(BF16)
    v_ref[0] = _dot(m, w_ref[:, D_MODEL:]).astype(BF16)


def _mem_kv(mem, g, wkv):
    B, M, D = mem.shape
    blk = pl.BlockSpec((1, M, D), lambda b: (b, 0, 0))
    return pl.pallas_call(
        _mem_kv_kernel,
        grid=(B,),
        in_specs=[blk, pl.BlockSpec((1, D), lambda b: (0, 0)), pl.BlockSpec(wkv.shape, lambda b: (0, 0))],
        out_specs=[blk, blk],
        out_shape=[jax.ShapeDtypeStruct((B, M, D), BF16)] * 2,
        compiler_params=_params(("parallel",)),
        name="mem_kv",
    )(mem, g, wkv)


def _cross_kernel(h_ref, k_ref, v_ref, wq_ref, wo_ref, gpre_ref, gpost_ref, o_ref):
    h = h_ref[0]
    n = _rms(h, gpre_ref[...]).astype(BF16)
    scale = CROSS_HEAD_DIM ** -0.5
    q_all = (_dot(n, wq_ref[...]) * scale).astype(BF16)
    heads = []
    for hh in range(CROSS_HEADS):
        sl = slice(hh * CROSS_HEAD_DIM, (hh + 1) * CROSS_HEAD_DIM)
        s = _dot_nt(q_all[:, sl], k_ref[0, :, sl])
        m = jnp.max(s, axis=-1, keepdims=True)
        p = jnp.exp(s - m)
        l = jnp.sum(p, axis=-1, keepdims=True)
        heads.append((_dot(p.astype(BF16), v_ref[0, :, sl]) / l).astype(BF16))
    out = _dot(jnp.concatenate(heads, axis=-1), wo_ref[...])
    o_ref[0] = h + _rms(out, gpost_ref[...])


def _cross(h, k, v, wq, wo, g_pre, g_post, tm=512):
    B, T, D = h.shape
    M = k.shape[1]
    row = pl.BlockSpec((1, tm, D), lambda b, i: (b, i, 0))
    kvb = pl.BlockSpec((1, M, D), lambda b, i: (b, 0, 0))
    wsp = pl.BlockSpec((D, D), lambda b, i: (0, 0))
    gsp = pl.BlockSpec((1, D), lambda b, i: (0, 0))
    return pl.pallas_call(
        _cross_kernel,
        grid=(B, T // tm),
        in_specs=[row, kvb, kvb, wsp, wsp, gsp, gsp],
        out_specs=row,
        out_shape=jax.ShapeDtypeStruct((B, T, D), F32),
        compiler_params=_params(("parallel", "parallel")),
        name="mem_cross",
    )(h, k, v, wq, wo, g_pre, g_post)


def _mlp_kernel(h_ref, wu_ref, wd_ref, gpre_ref, gpost_ref, o_ref, *, hc):
    h = h_ref[...]
    n = _rms(h, gpre_ref[...]).astype(BF16)
    acc = jnp.zeros(h.shape, F32)
    for c in range(wu_ref.shape[1] // hc):
        u = jnp.maximum(_dot(n, wu_ref[:, c * hc:(c + 1) * hc]), 0.0)
        acc = acc + _dot((u * u).astype(BF16), wd_ref[c * hc:(c + 1) * hc, :])
    o_ref[...] = h + _rms(acc, gpost_ref[...])


def _mlp(h, wu, wd, g_pre, g_post, tm=512, hc=512):
    N, D = h.shape
    row = pl.BlockSpec((tm, D), lambda i: (i, 0))
    gsp = pl.BlockSpec((1, D), lambda i: (0, 0))
    once = pl.Buffered(1)
    return pl.pallas_call(
        functools.partial(_mlp_kernel, hc=hc),
        grid=(N // tm,),
        in_specs=[row,
                  pl.BlockSpec(wu.shape, lambda i: (0, 0), pipeline_mode=once),
                  pl.BlockSpec(wd.shape, lambda i: (0, 0), pipeline_mode=once),
                  gsp, gsp],
        out_specs=row,
        out_shape=jax.ShapeDtypeStruct((N, D), F32),
        compiler_params=_params(("parallel",)),
        name="relu2_mlp",
    )(h, wu, wd, g_pre, g_post)


def _layer(h, mem, g_mix_pre, w_in, b_forget, w_ck1, w_ck2, w_cv1, w_cv2, pe_k, pe_v,
           w_mix_out, g_mix_post, g_x_pre, g_mem, w_xq, w_xkv, w_xo, g_x_post,
           g_mlp_pre, w_up, w_down, g_mlp_post):
    B, T, D = h.shape
    H, G, hd = NSA_HEADS, NSA_KV_HEADS, HEAD_DIM
    row = lambda g: g.reshape(1, -1)

    cols = {}
    lo = 0
    for name, size in (("fq", FOX_QKV), ("fk", FOX_QKV), ("fv", FOX_QKV), ("ff", FOX_HEADS), ("nq", NSA_Q),
                       ("kc", NSA_KV), ("vc", NSA_KV), ("ks", NSA_KV), ("vs", NSA_KV), ("kw", NSA_KV),
                       ("vw", NSA_KV), ("ng", NSA_HEADS * N_BRANCH)):
        cols[name] = w_in[:, lo:lo + size]
        lo += size
    ng_branch_major = cols["ng"].reshape(D, NSA_HEADS, N_BRANCH).transpose(0, 2, 1).reshape(D, -1)
    wt = jnp.concatenate([cols[k] for k in ("fq", "fk", "fv", "nq", "kc", "vc", "ks", "vs", "kw", "vw", "ff")]
                         + [ng_branch_major], axis=1).T.astype(BF16)
    half = hd // 2
    inv = ROPE_THETA ** (-jnp.arange(half, dtype=F32) / half)
    ang = inv[:, None] * jnp.arange(T, dtype=F32)[None, :]
    cos_t, sin_t = jnp.cos(ang), jnp.sin(ang)

    (fq_t, fk_t, fv_t, nq_t, nqr_t, kcvc_t, ks_t, vs_t, kw_t, vw_t, lf_t, gt_t) = _in_proj(
        h, row(g_mix_pre), wt, b_forget.reshape(FOX_HEADS, 1), cos_t, sin_t)

    t = 256
    nq = T // t

    def key_major(a_t, heads):
        return a_t.reshape(B, heads, hd, T).transpose(0, 1, 3, 2)

    def tile_major(a):
        return a.reshape(a.shape[:3] + (nq, t)).transpose(0, 1, 3, 2, 4)

    def token_major(o):
        return o.transpose(0, 2, 4, 1, 3).reshape(B, T, -1)

    c3 = _decay_cumsum(lf_t)
    c3h = c3.transpose(0, 2, 1, 3)
    ones_q = jnp.ones((B, FOX_HEADS, 3, T), BF16)
    qa_fox = jnp.concatenate([fq_t.reshape(B, FOX_HEADS, hd, T), c3h, ones_q,
                              jnp.zeros((B, FOX_HEADS, AUG - hd - 6, T), BF16)], axis=2)
    ka_fox = jnp.concatenate([key_major(fk_t, FOX_HEADS), ones_q.transpose(0, 1, 3, 2),
                              -c3h.transpose(0, 1, 3, 2),
                              jnp.zeros((B, FOX_HEADS, T, AUG - hd - 6), BF16)], axis=3)
    one_gate = jnp.ones((B, FOX_HEADS, nq, 1, t), F32)
    o_fox = _flash(tile_major(qa_fox), ka_fox, tile_major(fv_t.reshape(B, FOX_HEADS, hd, T)), one_gate, hb=4)

    n_chunk = T // CMP_STRIDE
    xc = (kcvc_t.reshape(B, 2 * G, hd, n_chunk, CMP_STRIDE).transpose(0, 1, 3, 4, 2)
          .reshape(B, 2 * G, n_chunk, CMP_STRIDE * hd))
    hsz = CMP_STRIDE * hd
    w1cat = jnp.stack([jnp.concatenate([w[:hsz], w[hsz:]], axis=1) for w in (w_ck1, w_cv1)]).astype(BF16)
    pe8 = jnp.stack([jnp.broadcast_to(p.reshape(1, -1), (8, CMP_LEN * hd)) for p in (pe_k, pe_v)])
    kc, vct = _compress(xc, w1cat, pe8, w_ck2.astype(BF16), w_cv2.T.astype(BF16))

    gates = gt_t.reshape(B, N_BRANCH, G, NSA_REP, T)
    n_sel = T // SEL_BLOCK
    o_cmp, selbias = _cmp_attn(nq_t, kc, vct, gates[:, 0:1], n_sel, t)

    qa_nsa = tile_major(jnp.concatenate([nqr_t.reshape(B, H, hd, T),
                                         jnp.repeat(selbias, NSA_REP, axis=1),
                                         jnp.zeros((B, H, AUG - hd - n_sel, T), BF16)], axis=2))
    onehot = (jnp.arange(T)[:, None] // SEL_BLOCK == jnp.arange(n_sel)[None, :]).astype(BF16)
    ka_slc = jnp.concatenate([key_major(ks_t, G), jnp.broadcast_to(onehot, (B, G, T, n_sel)),
                              jnp.zeros((B, G, T, AUG - hd - n_sel), BF16)], axis=3)
    gate_h = tile_major(gt_t.reshape(B * N_BRANCH, H, 1, T)).reshape(B, N_BRANCH, H, nq, 1, t)
    o_slc = _flash(qa_nsa, ka_slc, tile_major(vs_t.reshape(B, G, hd, T)), gate_h[:, 1], hb=NSA_REP)

    ka_win = jnp.concatenate([key_major(kw_t, G), jnp.zeros((B, G, T, AUG - hd), BF16)], axis=3)
    o_nsa = _flash(qa_nsa, ka_win, tile_major(vw_t.reshape(B, G, hd, T)), gate_h[:, 2], (o_cmp, o_slc),
                   hb=NSA_REP, window=WINDOW)

    h = _mix_out(h, token_major(o_fox), token_major(o_nsa), w_mix_out.astype(BF16), row(g_mix_post))

    k_mem, v_mem = _mem_kv(mem, row(g_mem), w_xkv.astype(BF16))
    h = _cross(h, k_mem, v_mem, w_xq.astype(BF16), w_xo.astype(BF16), row(g_x_pre), row(g_x_post))

    h = _mlp(h.reshape(B * T, D), w_up.astype(BF16), w_down.astype(BF16),
             row(g_mlp_pre), row(g_mlp_post)).reshape(B, T, D)
    return h


def kernel(x, mem, g_mix_pre, w_in, b_forget, w_ck1, w_ck2, w_cv1, w_cv2, pe_k, pe_v, w_mix_out, g_mix_post,
           g_x_pre, g_mem, w_xq, w_xkv, w_xo, g_x_post, g_mlp_pre, w_up, w_down, g_mlp_post):
    h = x
    for l in range(g_mix_pre.shape[0]):
        h = _layer(h, mem, g_mix_pre[l], w_in[l], b_forget[l], w_ck1[l], w_ck2[l], w_cv1[l], w_cv2[l],
                   pe_k[l], pe_v[l], w_mix_out[l], g_mix_post[l], g_x_pre[l], g_mem[l], w_xq[l], w_xkv[l],
                   w_xo[l], g_x_post[l], g_mlp_pre[l], w_up[l], w_down[l], g_mlp_post[l])
    return h
```

```python
import functools
import math

import jax
import jax.numpy as jnp
from jax import lax
from jax.experimental import pallas as pl
from jax.experimental.pallas import tpu as pltpu

D_MODEL = 1024
HEAD_DIM = 64
FOX_HEADS = 8
NSA_HEADS = 8
NSA_KV_HEADS = 2
NSA_REP = NSA_HEADS // NSA_KV_HEADS
CMP_LEN = 32
CMP_STRIDE = 16
CMP_HIDDEN = 2 * HEAD_DIM
SEL_BLOCK = 64
SEL_TOPK = 16
WINDOW = 512
N_BRANCH = 3
CROSS_HEADS = 4
CROSS_HEAD_DIM = D_MODEL // CROSS_HEADS
MLP_HIDDEN = 4 * D_MODEL
ROPE_THETA = 10000.0
RMS_EPS = 1e-6
FORCE_SCORE = 1e4
MASKED = -1e30
LOG2E = math.log2(math.e)

FOX_QKV = FOX_HEADS * HEAD_DIM
NSA_Q = NSA_HEADS * HEAD_DIM
NSA_KV = NSA_KV_HEADS * HEAD_DIM
AUG = 128

V7X_VMEM_LIMIT = 56 * 1024 * 1024

F32 = jnp.float32
BF16 = jnp.bfloat16


def _params(sem, vmem=V7X_VMEM_LIMIT):
    return pltpu.CompilerParams(dimension_semantics=sem, vmem_limit_bytes=vmem)


def _rms(x, g):
    return x * lax.rsqrt(jnp.mean(x * x, axis=-1, keepdims=True) + RMS_EPS) * g


def _dot(a, b):
    return jnp.dot(a, b, preferred_element_type=F32)


def _dot_nt(a, b):
    return lax.dot_general(a, b, (((1,), (1,)), ((), ())), preferred_element_type=F32)


def _split3(x):
    hi = x.astype(BF16)
    r1 = x - hi.astype(F32)
    mid = r1.astype(BF16)
    lo = (r1 - mid.astype(F32)).astype(BF16)
    return hi, mid, lo


_R_FQ, _R_FK, _R_FV, _R_NQ = 0, 512, 1024, 1536
_R_KC, _R_VC, _R_KS, _R_VS, _R_KW, _R_VW = 2048, 2176, 2304, 2432, 2560, 2688
_R_FF, _R_NG, _R_END = 2816, 2824, 2848


def _in_proj_kernel(x_ref, g_ref, wt_ref, bf_ref, cos_ref, sin_ref,
                    fq_ref, fk_ref, fv_ref, nq_ref, nqr_ref, kcvc_ref,
                    ks_ref, vs_ref, kw_ref, vw_ref, lf_ref, gt_ref, *, t):
    n = _rms(x_ref[0], g_ref[...]).astype(BF16)
    tm = n.shape[0]
    cos = cos_ref[...]
    sin = sin_ref[...]
    half = HEAD_DIM // 2

    def proj(lo, hi):
        return _dot_nt(wt_ref[lo:hi, :], n)

    def rope(r):
        parts = []
        for h in range(r.shape[0] // HEAD_DIM):
            x1 = r[h * HEAD_DIM:h * HEAD_DIM + half]
            x2 = r[h * HEAD_DIM + half:(h + 1) * HEAD_DIM]
            parts += [x1 * cos - x2 * sin, x2 * cos + x1 * sin]
        return jnp.concatenate(parts, axis=0)

    def store_tiles(r, out_ref):
        r = r.astype(BF16)
        for h in range(r.shape[0] // HEAD_DIM):
            for c in range(tm // t):
                out_ref[0, h, c] = r[h * HEAD_DIM:(h + 1) * HEAD_DIM, c * t:(c + 1) * t]

    scale = HEAD_DIM ** -0.5 * LOG2E
    store_tiles(proj(_R_FQ, _R_FK) * scale, fq_ref)
    fk_ref[0] = proj(_R_FK, _R_FV).astype(BF16)
    store_tiles(proj(_R_FV, _R_NQ), fv_ref)
    nq = proj(_R_NQ, _R_KC) * scale
    nq_ref[0] = nq.astype(BF16)
    store_tiles(rope(nq), nqr_ref)
    kcvc_ref[0] = proj(_R_KC, _R_KS).astype(BF16)
    ks_ref[0] = rope(proj(_R_KS, _R_VS)).astype(BF16)
    store_tiles(proj(_R_VS, _R_KW), vs_ref)
    kw_ref[0] = rope(proj(_R_KW, _R_VW)).astype(BF16)
    store_tiles(proj(_R_VW, _R_FF), vw_ref)
    small = proj(_R_FF, _R_END)
    z = small[0:FOX_HEADS] + bf_ref[...]
    lf_ref[0] = jnp.minimum(z, 0.0) - jnp.log1p(jnp.exp(-jnp.abs(z)))
    gt_ref[0] = jax.nn.sigmoid(small[FOX_HEADS:])


def _in_proj(x, g, wt, b_forget, cos_t, sin_t, t, tm=512):
    B, T, D = x.shape
    grid = (B, T // tm)
    chan = lambda c, dt: (jax.ShapeDtypeStruct((B, c, T), dt), pl.BlockSpec((1, c, tm), lambda b, i: (b, 0, i)))
    tiles = lambda h: (jax.ShapeDtypeStruct((B, h, T // t, HEAD_DIM, t), BF16),
                       pl.BlockSpec((1, h, tm // t, HEAD_DIM, t), lambda b, i: (b, 0, i, 0, 0)))
    outs = [tiles(FOX_HEADS), chan(FOX_QKV, BF16), tiles(FOX_HEADS), chan(NSA_Q, BF16), tiles(NSA_HEADS),
            chan(2 * NSA_KV, BF16), chan(NSA_KV, BF16), tiles(NSA_KV_HEADS), chan(NSA_KV, BF16),
            tiles(NSA_KV_HEADS), chan(FOX_HEADS, F32), chan(NSA_HEADS * N_BRANCH, F32)]
    return pl.pallas_call(
        functools.partial(_in_proj_kernel, t=t),
        grid=grid,
        in_specs=[
            pl.BlockSpec((1, tm, D), lambda b, i: (b, i, 0)),
            pl.BlockSpec((1, D), lambda b, i: (0, 0)),
            pl.BlockSpec((_R_END, D), lambda b, i: (0, 0)),
            pl.BlockSpec((FOX_HEADS, 1), lambda b, i: (0, 0)),
            pl.BlockSpec((HEAD_DIM // 2, tm), lambda b, i: (0, i)),
            pl.BlockSpec((HEAD_DIM // 2, tm), lambda b, i: (0, i)),
        ],
        out_specs=[o[1] for o in outs],
        out_shape=[o[0] for o in outs],
        compiler_params=_params(("parallel", "parallel")),
        name="in_proj",
    )(x, g, wt, b_forget, cos_t, sin_t)


_CS = 128


def _cumsum_kernel(x_ref, o_ref):
    x = x_ref[0]
    r = lax.broadcasted_iota(jnp.int32, (_CS, _CS), 0)
    c = lax.broadcasted_iota(jnp.int32, (_CS, _CS), 1)
    tri = (r <= c).astype(BF16)
    xh, xm, xl = _split3(x)
    y = _dot(xh, tri) + _dot(xm, tri) + _dot(xl, tri)
    sh = (_CS // FOX_HEADS).bit_length() - 1
    prev = ((jnp.right_shift(r, sh) == jnp.right_shift(c, sh)) & (c < r)).astype(BF16)
    tot = jnp.broadcast_to(y[:, _CS - 1:_CS], (_CS, _CS))
    th, tm_, tl = _split3(tot)
    carry = _dot(prev, th) + _dot(prev, tm_) + _dot(prev, tl)
    ch, cm, cl = _split3((y + carry) * LOG2E)
    o_ref[0, 0] = ch
    o_ref[0, 1] = cm
    o_ref[0, 2] = cl


def _decay_cumsum(lf):
    B, H, T = lf.shape
    assert H * (T // _CS) == _CS
    x = lf.reshape(B, _CS, _CS)
    out = pl.pallas_call(
        _cumsum_kernel,
        grid=(B,),
        in_specs=[pl.BlockSpec((1, _CS, _CS), lambda b: (b, 0, 0))],
        out_specs=pl.BlockSpec((1, 3, _CS, _CS), lambda b: (b, 0, 0, 0)),
        out_shape=jax.ShapeDtypeStruct((B, 3, _CS, _CS), BF16),
        compiler_params=_params(("parallel",)),
        name="decay_cumsum",
    )(x)
    return out.reshape(B, 3, H, T)


def _flash_kernel(q_ref, k_ref, v_ref, mask_ref, g_ref, *rest, t, back, n_aug, n_extra):
    aug_ref = rest[0] if n_aug else None
    extras = rest[n_aug:n_aug + n_extra]
    o_ref, s_ref, p_ref = rest[n_aug + n_extra:]
    hb, nq, hd = q_ref.shape[1], q_ref.shape[2], q_ref.shape[3]
    shared = k_ref.shape[1] == 1
    n_pairs = sum(min(i, back) + 1 for i in range(nq))

    def masked_scores(h, i, j):
        rows = [q_ref[0, h, i]]
        if aug_ref is not None:
            rows.append(aug_ref[0, h if aug_ref.shape[1] == hb else 0, i])
        pad = AUG - sum(r.shape[0] for r in rows)
        q = jnp.concatenate(rows + [jnp.zeros((pad, t), BF16)], axis=0)
        k = k_ref[0, 0 if shared else h, pl.ds(pl.multiple_of(j * t, t), t), :]
        s = _dot(k, q) + mask_ref[jnp.maximum(j - i + (mask_ref.shape[0] - 1), 0)]
        s_ref[h] = s
        return jnp.max(s, axis=0, keepdims=True)

    def weighted_values(h, j):
        return _dot(v_ref[0, 0 if shared else h, j], p_ref[h])

    def emit(h, i, l, acc):
        o_ref[0, h, i] = (acc * (g_ref[0, h, i] / l)).astype(o_ref.dtype)

    tile_max = []
    for h in range(hb):
        tile_max.append(masked_scores(h, 0, 0))
        p_ref[h] = jnp.zeros((t, t), BF16)

    def body(_, carry):
        (i, j, i_prev, j_prev), heads = carry
        row_end = j == i
        i_next = jnp.minimum(jnp.where(row_end, i + 1, i), nq - 1)
        j_next = jnp.where(row_end, jnp.maximum(i_next - back, 0), j + 1)
        row_start = j == jnp.maximum(i - back, 0)
        out = []
        for h in range(hb):
            m, l, acc, alpha, s_max = heads[h]
            acc = acc * alpha + weighted_values(h, j_prev)
            emit(h, i_prev, l, acc)
            m = jnp.where(row_start, MASKED, m)
            m_new = jnp.maximum(m, s_max)
            alpha = jnp.exp2(m - m_new)
            p = jnp.exp2(s_ref[h] - m_new)
            l = alpha * l + jnp.sum(p, axis=0, keepdims=True)
            p_ref[h] = p.astype(BF16)
            s_max = masked_scores(h, i_next, j_next)
            out.append((m_new, l, acc, alpha, s_max))
        return (i_next, j_next, i, j), tuple(out)

    zero = jnp.int32(0)
    heads = tuple((jnp.full((1, t), MASKED, F32), jnp.ones((1, t), F32), jnp.zeros((hd, t), F32),
                   jnp.ones((1, t), F32), tile_max[h]) for h in range(hb))
    _, heads = lax.fori_loop(0, n_pairs, body, ((zero, zero, zero, zero), heads))
    for h in range(hb):
        m, l, acc, alpha, _ = heads[h]
        emit(h, nq - 1, l, acc * alpha + weighted_values(h, nq - 1))
    for x_ref in extras:
        for h in range(hb):
            for i in range(nq):
                o_ref[0, h, i] = (o_ref[0, h, i].astype(F32) + x_ref[0, h, i].astype(F32)).astype(o_ref.dtype)


def _mask_table(t, n, window):
    k = jnp.arange(t)[:, None]
    q = jnp.arange(t)[None, :]
    tiles = []
    for d in range(n):
        off = (d - (n - 1)) * t
        ok = (k + off) <= q
        if window is not None:
            ok = ok & ((k + off) > q - window)
        tiles.append(ok)
    return jnp.where(jnp.stack(tiles), 0.0, MASKED).astype(F32)


def _flash(q, ka, v_t, gate, aug=None, extras=(), *, hb, window=None):
    B, H, nq, hd, t = q.shape
    G = v_t.shape[1]
    T = nq * t
    kb = hb if G == H else 1
    assert G == H or hb == H // G
    if window is None:
        back = nq - 1
        masks = _mask_table(t, 2, None)
    else:
        assert window % t == 0
        back = window // t
        masks = _mask_table(t, back + 1, window)
    tile = lambda n, c: pl.BlockSpec((1, n, nq, c, t), lambda b, h: (b, h, 0, 0, 0))
    augs = () if aug is None else (aug,)
    aug_specs = [tile(hb if a.shape[1] == H else 1, a.shape[3]) for a in augs]
    return pl.pallas_call(
        functools.partial(_flash_kernel, t=t, back=back, n_aug=len(augs), n_extra=len(extras)),
        grid=(B, H // hb),
        in_specs=[
            tile(hb, hd),
            pl.BlockSpec((1, kb, T, AUG), lambda b, h: (b, h, 0, 0)),
            tile(kb, hd),
            pl.BlockSpec(masks.shape, lambda b, h: (0, 0, 0)),
            tile(hb, 1),
        ] + aug_specs + [tile(hb, hd)] * len(extras),
        out_specs=tile(hb, hd),
        out_shape=jax.ShapeDtypeStruct((B, H, nq, hd, t), BF16),
        scratch_shapes=[pltpu.VMEM((hb, t, t), F32), pltpu.VMEM((hb, t, t), BF16)],
        compiler_params=_params(("parallel", "parallel")),
        name="flash_window" if window is not None else "flash_causal",
    )(q, ka, v_t, masks, gate, *augs, *extras)


def _compress_kernel(x_ref, w1_ref, pe_ref, w2k_ref, w2vt_ref, kc_ref, vct_ref):
    half = CMP_STRIDE * HEAD_DIM
    n_chunk = x_ref.shape[2]
    for kind in range(2):
        w1 = w1_ref[kind]
        pe = pe_ref[kind].astype(BF16)
        bias = (_dot(pe[:, :half], w1[:, :CMP_HIDDEN]) + _dot(pe[:, half:], w1[:, CMP_HIDDEN:]))[0:1]
        for g in range(NSA_KV_HEADS):
            ab = _dot(x_ref[0, kind * NSA_KV_HEADS + g], w1)
            nxt = pltpu.roll(ab[:, CMP_HIDDEN:], n_chunk - 1, 0)
            pre = ab[:, :CMP_HIDDEN] + nxt + bias
            hdn = (pre * jax.nn.sigmoid(pre)).astype(BF16)
            if kind == 0:
                kc_ref[0, g] = _dot(hdn, w2k_ref[...]).astype(BF16)
            else:
                vct_ref[0, g] = _dot_nt(w2vt_ref[...], hdn).astype(BF16)


def _compress(xc, w1cat, pe8, w2k, w2vt):
    B = xc.shape[0]
    n_chunk = xc.shape[2]
    full = lambda a: pl.BlockSpec(a.shape, lambda b: (0,) * a.ndim)
    return pl.pallas_call(
        _compress_kernel,
        grid=(B,),
        in_specs=[pl.BlockSpec((1,) + xc.shape[1:], lambda b: (b, 0, 0, 0)),
                  full(w1cat), full(pe8), full(w2k), full(w2vt)],
        out_specs=[pl.BlockSpec((1, NSA_KV_HEADS, n_chunk, HEAD_DIM), lambda b: (b, 0, 0, 0)),
                   pl.BlockSpec((1, NSA_KV_HEADS, HEAD_DIM, n_chunk), lambda b: (b, 0, 0, 0))],
        out_shape=[jax.ShapeDtypeStruct((B, NSA_KV_HEADS, n_chunk, HEAD_DIM), BF16),
                   jax.ShapeDtypeStruct((B, NSA_KV_HEADS, HEAD_DIM, n_chunk), BF16)],
        compiler_params=_params(("parallel",)),
        name="nsa_compress",
    )(xc, w1cat, pe8, w2k, w2vt)


def _cmp_attn_kernel(q_ref, kc_ref, vct_ref, g_ref, o_ref, sb_ref, *, tq, t):
    i = pl.program_id(2)
    n_cmp = kc_ref.shape[2]
    n_sel = sb_ref.shape[3]
    n_io = lax.broadcasted_iota(jnp.int32, (n_cmp, tq), 0)
    t_io = i * tq + lax.broadcasted_iota(jnp.int32, (n_cmp, tq), 1)
    valid = n_io * CMP_STRIDE + (CMP_LEN - 1) <= t_io
    kc = kc_ref[0, 0]
    vct = vct_ref[0, 0]
    psum = jnp.zeros((n_cmp, tq), F32)
    for r in range(NSA_REP):
        q = q_ref[0, r * HEAD_DIM:(r + 1) * HEAD_DIM, :]
        s = jnp.where(valid, _dot(kc, q), MASKED)
        m = jnp.max(s, axis=0, keepdims=True)
        e = jnp.where(valid, jnp.exp2(s - m), 0.0)
        l = jnp.sum(e, axis=0, keepdims=True)
        p = e * jnp.where(l > 0.0, 1.0 / l, 0.0)
        psum = psum + p
        o = _dot(vct, p.astype(BF16))
        o = (o * g_ref[0, 0, 0, r:r + 1, :]).astype(BF16)
        for c in range(tq // t):
            o_ref[0, r, c] = o[:, c * t:(c + 1) * t]

    jr = lax.broadcasted_iota(jnp.int32, (n_sel, n_cmp), 0)
    nc = lax.broadcasted_iota(jnp.int32, (n_sel, n_cmp), 1)
    overlap = ((nc * CMP_STRIDE < (jr + 1) * SEL_BLOCK)
               & (nc * CMP_STRIDE + CMP_LEN > jr * SEL_BLOCK)).astype(BF16)
    ph, pm, pl_ = _split3(psum)
    imp = _dot(overlap, ph) + _dot(overlap, pm) + _dot(overlap, pl_)
    j_io = lax.broadcasted_iota(jnp.int32, (n_sel, tq), 0)
    cur = jnp.right_shift(i * tq + lax.broadcasted_iota(jnp.int32, (n_sel, tq), 1),
                          SEL_BLOCK.bit_length() - 1)
    is_cur = j_io == cur
    is_fixed = (j_io == 0) | (j_io == cur - 1)
    imp = jnp.where(is_cur, 2.0 * FORCE_SCORE, jnp.where(is_fixed, FORCE_SCORE, imp))
    imp = jnp.where(j_io <= cur, imp, -1.0)
    cnt = jnp.zeros((n_sel, tq), jnp.int32)
    for jp in range(n_sel):
        row = imp[jp:jp + 1, :]
        beats = (row > imp) | ((row == imp) & (j_io > jp))
        cnt = cnt + beats.astype(jnp.int32)
    sb = jnp.where(cnt < min(SEL_TOPK, n_sel), 0.0, MASKED).astype(BF16)
    for c in range(tq // t):
        sb_ref[0, 0, c] = sb[:, c * t:(c + 1) * t]


def _cmp_attn(nq_t, kc, vct, gate_cmp, n_sel, t, tq=1024):
    B, _, T = nq_t.shape
    G = kc.shape[1]
    rows = NSA_REP * HEAD_DIM
    return pl.pallas_call(
        functools.partial(_cmp_attn_kernel, tq=tq, t=t),
        grid=(B, G, T // tq),
        in_specs=[
            pl.BlockSpec((1, rows, tq), lambda b, g, i: (b, g, i)),
            pl.BlockSpec((1, 1) + kc.shape[2:], lambda b, g, i: (b, g, 0, 0)),
            pl.BlockSpec((1, 1) + vct.shape[2:], lambda b, g, i: (b, g, 0, 0)),
            pl.BlockSpec((1, 1, 1, NSA_REP, tq), lambda b, g, i: (b, 0, g, 0, i)),
        ],
        out_specs=[pl.BlockSpec((1, NSA_REP, tq // t, HEAD_DIM, t), lambda b, g, i: (b, g, i, 0, 0)),
                   pl.BlockSpec((1, 1, tq // t, n_sel, t), lambda b, g, i: (b, g, i, 0, 0))],
        out_shape=[jax.ShapeDtypeStruct((B, G * NSA_REP, T // t, HEAD_DIM, t), BF16),
                   jax.ShapeDtypeStruct((B, G, T // t, n_sel, t), BF16)],
        compiler_params=_params(("parallel", "parallel", "parallel")),
        name="nsa_cmp_attn",
    )(nq_t, kc, vct, gate_cmp)


def _mix_out_kernel(x_ref, of_ref, on_ref, wot_ref, g_ref, o_ref):
    heads, n_tiles, hd, t = of_ref.shape[1:]
    for c in range(n_tiles):
        of = of_ref[0, :, c].reshape(heads * hd, t)
        on = on_ref[0, :, c].reshape(heads * hd, t)
        mix_t = _dot(wot_ref[:, 0:FOX_QKV], of) + _dot(wot_ref[:, FOX_QKV:], on)
        rows = slice(c * t, (c + 1) * t)
        o_ref[0, rows, :] = x_ref[0, rows, :] + _rms(mix_t.T, g_ref[...])


def _mix_out(x, o_fox, o_nsa, wot, g, tm=512):
    B, T, D = x.shape
    heads, _, hd, t = o_fox.shape[1:]
    row = pl.BlockSpec((1, tm, D), lambda b, i: (b, i, 0))
    tiles = pl.BlockSpec((1, heads, tm // t, hd, t), lambda b, i: (b, 0, i, 0, 0))
    return pl.pallas_call(
        _mix_out_kernel,
        grid=(B, T // tm),
        in_specs=[row, tiles, tiles,
                  pl.BlockSpec(wot.shape, lambda b, i: (0, 0)),
                  pl.BlockSpec((1, D), lambda b, i: (0, 0))],
        out_specs=row,
        out_shape=jax.ShapeDtypeStruct((B, T, D), F32),
        compiler_params=_params(("parallel", "parallel")),
        name="mix_out",
    )(x, o_fox, o_nsa, wot, g)


def _mem_kv_kernel(m_ref, g_ref, w_ref, k_ref, v_ref):
    m = _rms(m_ref[0], g_ref[...]).astype(BF16)
    k_ref[0] = _dot(m, w_ref[:, :D_MODEL]).astype(BF16)
    v_ref[0] = _dot(m, w_ref[:, D_MODEL:]).astype(BF16)


def _mem_kv(mem, g, wkv):
    B, M, D = mem.shape
    blk = pl.BlockSpec((1, M, D), lambda b: (b, 0, 0))
    return pl.pallas_call(
        _mem_kv_kernel,
        grid=(B,),
        in_specs=[blk, pl.BlockSpec((1, D), lambda b: (0, 0)), pl.BlockSpec(wkv.shape, lambda b: (0, 0))],
        out_specs=[blk, blk],
        out_shape=[jax.ShapeDtypeStruct((B, M, D), BF16)] * 2,
        compiler_params=_params(("parallel",)),
        name="mem_kv",
    )(mem, g, wkv)


def _cross_kernel(h_ref, k_ref, v_ref, wq_ref, wo_ref, gpre_ref, gpost_ref, o_ref):
    h = h_ref[0]
    n = _rms(h, gpre_ref[...]).astype(BF16)
    scale = CROSS_HEAD_DIM ** -0.5
    q_all = (_dot(n, wq_ref[...]) * scale).astype(BF16)
    heads = []
    for hh in range(CROSS_HEADS):
        sl = slice(hh * CROSS_HEAD_DIM, (hh + 1) * CROSS_HEAD_DIM)
        s = _dot_nt(q_all[:, sl], k_ref[0, :, sl])
        m = jnp.max(s, axis=-1, keepdims=True)
        p = jnp.exp(s - m)
        l = jnp.sum(p, axis=-1, keepdims=True)
        heads.append((_dot(p.astype(BF16), v_ref[0, :, sl]) / l).astype(BF16))
    out = _dot(jnp.concatenate(heads, axis=-1), wo_ref[...])
    o_ref[0] = h + _rms(out, gpost_ref[...])


def _cross(h, k, v, wq, wo, g_pre, g_post, tm=512):
    B, T, D = h.shape
    M = k.shape[1]
    row = pl.BlockSpec((1, tm, D), lambda b, i: (b, i, 0))
    kvb = pl.BlockSpec((1, M, D), lambda b, i: (b, 0, 0))
    wsp = pl.BlockSpec((D, D), lambda b, i: (0, 0))
    gsp = pl.BlockSpec((1, D), lambda b, i: (0, 0))
    return pl.pallas_call(
        _cross_kernel,
        grid=(B, T // tm),
        in_specs=[row, kvb, kvb, wsp, wsp, gsp, gsp],
        out_specs=row,
        out_shape=jax.ShapeDtypeStruct((B, T, D), F32),
        compiler_params=_params(("parallel", "parallel")),
        name="mem_cross",
    )(h, k, v, wq, wo, g_pre, g_post)


def _mlp_kernel(h_ref, wu_ref, wd_ref, gpre_ref, gpost_ref, o_ref, *, hc):
    h = h_ref[...]
    n = _rms(h, gpre_ref[...]).astype(BF16)
    acc = jnp.zeros(h.shape, F32)
    for c in range(wu_ref.shape[1] // hc):
        u = jnp.maximum(_dot(n, wu_ref[:, c * hc:(c + 1) * hc]), 0.0)
        acc = acc + _dot((u * u).astype(BF16), wd_ref[c * hc:(c + 1) * hc, :])
    o_ref[...] = h + _rms(acc, gpost_ref[...])


def _mlp(h, wu, wd, g_pre, g_post, tm=512, hc=512):
    N, D = h.shape
    row = pl.BlockSpec((tm, D), lambda i: (i, 0))
    gsp = pl.BlockSpec((1, D), lambda i: (0, 0))
    once = pl.Buffered(1)
    return pl.pallas_call(
        functools.partial(_mlp_kernel, hc=hc),
        grid=(N // tm,),
        in_specs=[row,
                  pl.BlockSpec(wu.shape, lambda i: (0, 0), pipeline_mode=once),
                  pl.BlockSpec(wd.shape, lambda i: (0, 0), pipeline_mode=once),
                  gsp, gsp],
        out_specs=row,
        out_shape=jax.ShapeDtypeStruct((N, D), F32),
        compiler_params=_params(("parallel",)),
        name="relu2_mlp",
    )(h, wu, wd, g_pre, g_post)


def _layer(h, mem, g_mix_pre, w_in, b_forget, w_ck1, w_ck2, w_cv1, w_cv2, pe_k, pe_v,
           w_mix_out, g_mix_post, g_x_pre, g_mem, w_xq, w_xkv, w_xo, g_x_post,
           g_mlp_pre, w_up, w_down, g_mlp_post):
    B, T, D = h.shape
    H, G, hd = NSA_HEADS, NSA_KV_HEADS, HEAD_DIM
    row = lambda g: g.reshape(1, -1)

    cols = {}
    lo = 0
    for name, size in (("fq", FOX_QKV), ("fk", FOX_QKV), ("fv", FOX_QKV), ("ff", FOX_HEADS), ("nq", NSA_Q),
                       ("kc", NSA_KV), ("vc", NSA_KV), ("ks", NSA_KV), ("vs", NSA_KV), ("kw", NSA_KV),
                       ("vw", NSA_KV), ("ng", NSA_HEADS * N_BRANCH)):
        cols[name] = w_in[:, lo:lo + size]
        lo += size
    ng_branch_major = cols["ng"].reshape(D, NSA_HEADS, N_BRANCH).transpose(0, 2, 1).reshape(D, -1)
    wt = jnp.concatenate([cols[k] for k in ("fq", "fk", "fv", "nq", "kc", "vc", "ks", "vs", "kw", "vw", "ff")]
                         + [ng_branch_major], axis=1).T.astype(BF16)
    half = hd // 2
    inv = ROPE_THETA ** (-jnp.arange(half, dtype=F32) / half)
    ang = inv[:, None] * jnp.arange(T, dtype=F32)[None, :]
    cos_t, sin_t = jnp.cos(ang), jnp.sin(ang)

    t = 256
    nq = T // t
    (fq, fk_t, fv, nq_t, nqr, kcvc_t, ks_t, vs, kw_t, vw, lf_t, gt_t) = _in_proj(
        h, row(g_mix_pre), wt, b_forget.reshape(FOX_HEADS, 1), cos_t, sin_t, t)

    def key_major(a_t, heads):
        return a_t.reshape(B, heads, hd, T).transpose(0, 1, 3, 2)

    def tile_major(a):
        return a.reshape(a.shape[:3] + (nq, t)).transpose(0, 1, 3, 2, 4)

    c3 = _decay_cumsum(lf_t)
    c3h = c3.transpose(0, 2, 1, 3)
    ones_q = jnp.ones((B, FOX_HEADS, 3, T), BF16)
    aug_fox = tile_major(jnp.concatenate([c3h, ones_q, jnp.zeros((B, FOX_HEADS, 10, T), BF16)], axis=2))
    ka_fox = jnp.concatenate([key_major(fk_t, FOX_HEADS), ones_q.transpose(0, 1, 3, 2),
                              -c3h.transpose(0, 1, 3, 2),
                              jnp.zeros((B, FOX_HEADS, T, AUG - hd - 6), BF16)], axis=3)
    one_gate = jnp.ones((B, FOX_HEADS, nq, 1, t), F32)
    o_fox = _flash(fq, ka_fox, fv, one_gate, aug_fox, hb=4)

    n_chunk = T // CMP_STRIDE
    xc = (kcvc_t.reshape(B, 2 * G, hd, n_chunk, CMP_STRIDE).transpose(0, 1, 3, 4, 2)
          .reshape(B, 2 * G, n_chunk, CMP_STRIDE * hd))
    hsz = CMP_STRIDE * hd
    w1cat = jnp.stack([jnp.concatenate([w[:hsz], w[hsz:]], axis=1) for w in (w_ck1, w_cv1)]).astype(BF16)
    pe8 = jnp.stack([jnp.broadcast_to(p.reshape(1, -1), (8, CMP_LEN * hd)) for p in (pe_k, pe_v)])
    kc, vct = _compress(xc, w1cat, pe8, w_ck2.astype(BF16), w_cv2.T.astype(BF16))

    gates = gt_t.reshape(B, N_BRANCH, G, NSA_REP, T)
    n_sel = T // SEL_BLOCK
    o_cmp, selbias = _cmp_attn(nq_t, kc, vct, gates[:, 0:1], n_sel, t)

    onehot = (jnp.arange(T)[:, None] // SEL_BLOCK == jnp.arange(n_sel)[None, :]).astype(BF16)
    ka_slc = jnp.concatenate([key_major(ks_t, G), jnp.broadcast_to(onehot, (B, G, T, n_sel)),
                              jnp.zeros((B, G, T, AUG - hd - n_sel), BF16)], axis=3)
    gate_h = tile_major(gt_t.reshape(B * N_BRANCH, H, 1, T)).reshape(B, N_BRANCH, H, nq, 1, t)
    o_slc = _flash(nqr, ka_slc, vs, gate_h[:, 1], selbias, hb=NSA_REP)

    ka_win = jnp.concatenate([key_major(kw_t, G), jnp.zeros((B, G, T, AUG - hd), BF16)], axis=3)
    o_nsa = _flash(nqr, ka_win, vw, gate_h[:, 2], extras=(o_cmp, o_slc), hb=NSA_REP, window=WINDOW)

    h = _mix_out(h, o_fox, o_nsa, w_mix_out.T.astype(BF16), row(g_mix_post))

    k_mem, v_mem = _mem_kv(mem, row(g_mem), w_xkv.astype(BF16))
    h = _cross(h, k_mem, v_mem, w_xq.astype(BF16), w_xo.astype(BF16), row(g_x_pre), row(g_x_post))

    h = _mlp(h.reshape(B * T, D), w_up.astype(BF16), w_down.astype(BF16),
             row(g_mlp_pre), row(g_mlp_post)).reshape(B, T, D)
    return h


def kernel(x, mem, g_mix_pre, w_in, b_forget, w_ck1, w_ck2, w_cv1, w_cv2, pe_k, pe_v, w_mix_out, g_mix_post,
           g_x_pre, g_mem, w_xq, w_xkv, w_xo, g_x_post, g_mlp_pre, w_up, w_down, g_mlp_post):
    h = x
    for l in range(g_mix_pre.shape[0]):
        h = _layer(h, mem, g_mix_pre[l], w_in[l], b_forget[l], w_ck1[l], w_ck2[l], w_cv1[l], w_cv2[l],
                   pe_k[l], pe_v[l], w_mix_out[l], g_mix_post[l], g_x_pre[l], g_mem[l], w_xq[l], w_xkv[l],
                   w_xo[l], g_x_post[l], g_mlp_pre[l], w_up[l], w_down[l], g_mlp_post[l])
    return h
```

```python
import functools
import math

import jax
import jax.numpy as jnp
from jax import lax
from jax.experimental import pallas as pl
from jax.experimental.pallas import tpu as pltpu

D_MODEL = 1024
HEAD_DIM = 64
FOX_HEADS = 8
NSA_HEADS = 8
NSA_KV_HEADS = 2
NSA_REP = NSA_HEADS // NSA_KV_HEADS
CMP_LEN = 32
CMP_STRIDE = 16
CMP_HIDDEN = 2 * HEAD_DIM
SEL_BLOCK = 64
SEL_TOPK = 16
WINDOW = 512
N_BRANCH = 3
CROSS_HEADS = 4
CROSS_HEAD_DIM = D_MODEL // CROSS_HEADS
MLP_HIDDEN = 4 * D_MODEL
ROPE_THETA = 10000.0
RMS_EPS = 1e-6
FORCE_SCORE = 1e4
MASKED = -1e30
LOG2E = math.log2(math.e)

FOX_QKV = FOX_HEADS * HEAD_DIM
NSA_Q = NSA_HEADS * HEAD_DIM
NSA_KV = NSA_KV_HEADS * HEAD_DIM
AUG = 128

V7X_VMEM_LIMIT = 56 * 1024 * 1024

F32 = jnp.float32
BF16 = jnp.bfloat16


def _params(sem, vmem=V7X_VMEM_LIMIT):
    return pltpu.CompilerParams(dimension_semantics=sem, vmem_limit_bytes=vmem)


def _rms(x, g):
    return x * lax.rsqrt(jnp.mean(x * x, axis=-1, keepdims=True) + RMS_EPS) * g


def _dot(a, b):
    return jnp.dot(a, b, preferred_element_type=F32)


def _dot_nt(a, b):
    return lax.dot_general(a, b, (((1,), (1,)), ((), ())), preferred_element_type=F32)


def _split3(x):
    hi = x.astype(BF16)
    r1 = x - hi.astype(F32)
    mid = r1.astype(BF16)
    lo = (r1 - mid.astype(F32)).astype(BF16)
    return hi, mid, lo


_R_FQ, _R_FV, _R_NQ, _R_KS, _R_VS, _R_KW, _R_VW = 0, 512, 1024, 1536, 1664, 1792, 1920
_R_FF, _R_NG, _R_END = 2048, 2056, 2080
V_ROWS = HEAD_DIM + 16


def _in_proj_kernel(x_ref, g_ref, wt_ref, wk_ref, bf_ref, cos_ref, sin_ref,
                    fq_ref, fk_ref, fv_ref, nq_ref, nqr_ref, kcvc_ref,
                    ks_ref, vs_ref, kw_ref, vw_ref, lf_ref, gt_ref, *, t):
    n = _rms(x_ref[0], g_ref[...]).astype(BF16)
    tm = n.shape[0]
    cos = cos_ref[...]
    sin = sin_ref[...]
    half = HEAD_DIM // 2

    def proj(lo, hi):
        return _dot_nt(wt_ref[lo:hi, :], n)

    def rope(r):
        parts = []
        for h in range(r.shape[0] // HEAD_DIM):
            x1 = r[h * HEAD_DIM:h * HEAD_DIM + half]
            x2 = r[h * HEAD_DIM + half:(h + 1) * HEAD_DIM]
            parts += [x1 * cos - x2 * sin, x2 * cos + x1 * sin]
        return jnp.concatenate(parts, axis=0)

    def store_tiles(r, out_ref):
        r = r.astype(BF16)
        rows = out_ref.shape[3]
        if rows > HEAD_DIM:
            sub = lax.broadcasted_iota(jnp.int32, (rows - HEAD_DIM, t), 0)
            tail = jnp.where(sub == 0, 1.0, 0.0).astype(BF16)
        for h in range(r.shape[0] // HEAD_DIM):
            for c in range(tm // t):
                tile = r[h * HEAD_DIM:(h + 1) * HEAD_DIM, c * t:(c + 1) * t]
                out_ref[0, h, c] = tile if rows == HEAD_DIM else jnp.concatenate([tile, tail], axis=0)

    scale = HEAD_DIM ** -0.5 * LOG2E
    store_tiles(proj(_R_FQ, _R_FV) * scale, fq_ref)
    store_tiles(proj(_R_FV, _R_NQ), fv_ref)
    nq = proj(_R_NQ, _R_KS) * scale
    nq_ref[0] = nq.astype(BF16)
    store_tiles(rope(nq), nqr_ref)
    ks_ref[0] = rope(proj(_R_KS, _R_VS)).astype(BF16)
    store_tiles(proj(_R_VS, _R_KW), vs_ref)
    kw_ref[0] = rope(proj(_R_KW, _R_VW)).astype(BF16)
    store_tiles(proj(_R_VW, _R_FF), vw_ref)
    small = proj(_R_FF, _R_END)
    z = small[0:FOX_HEADS] + bf_ref[...]
    lf_ref[0] = jnp.minimum(z, 0.0) - jnp.log1p(jnp.exp(-jnp.abs(z)))
    gt_ref[0] = jax.nn.sigmoid(small[FOX_HEADS:])
    tok = _dot(n, wk_ref[...])
    fk_ref[0] = tok[:, :FOX_QKV].astype(BF16)
    kcvc_ref[0] = tok[:, FOX_QKV:].astype(BF16)


def _in_proj(x, g, wt, wk, b_forget, cos_t, sin_t, t, tm=512):
    B, T, D = x.shape
    grid = (B, T // tm)
    chan = lambda c, dt: (jax.ShapeDtypeStruct((B, c, T), dt), pl.BlockSpec((1, c, tm), lambda b, i: (b, 0, i)))
    toks = lambda c: (jax.ShapeDtypeStruct((B, T, c), BF16), pl.BlockSpec((1, tm, c), lambda b, i: (b, i, 0)))
    tiles = lambda h, rows: (jax.ShapeDtypeStruct((B, h, T // t, rows, t), BF16),
                             pl.BlockSpec((1, h, tm // t, rows, t), lambda b, i: (b, 0, i, 0, 0)))
    outs = [tiles(FOX_HEADS, HEAD_DIM), toks(FOX_QKV), tiles(FOX_HEADS, V_ROWS), chan(NSA_Q, BF16),
            tiles(NSA_HEADS, HEAD_DIM), toks(2 * NSA_KV), chan(NSA_KV, BF16), tiles(NSA_KV_HEADS, V_ROWS),
            chan(NSA_KV, BF16), tiles(NSA_KV_HEADS, V_ROWS), chan(FOX_HEADS, F32), chan(NSA_HEADS * N_BRANCH, F32)]
    return pl.pallas_call(
        functools.partial(_in_proj_kernel, t=t),
        grid=grid,
        in_specs=[
            pl.BlockSpec((1, tm, D), lambda b, i: (b, i, 0)),
            pl.BlockSpec((1, D), lambda b, i: (0, 0)),
            pl.BlockSpec((_R_END, D), lambda b, i: (0, 0)),
            pl.BlockSpec(wk.shape, lambda b, i: (0, 0)),
            pl.BlockSpec((FOX_HEADS, 1), lambda b, i: (0, 0)),
            pl.BlockSpec((HEAD_DIM // 2, tm), lambda b, i: (0, i)),
            pl.BlockSpec((HEAD_DIM // 2, tm), lambda b, i: (0, i)),
        ],
        out_specs=[o[1] for o in outs],
        out_shape=[o[0] for o in outs],
        compiler_params=_params(("parallel", "parallel")),
        name="in_proj",
    )(x, g, wt, wk, b_forget, cos_t, sin_t)


_CS = 128
Q_AUG_FOX = 16


def _decay_kernel(x_ref, qa_ref, ka_ref, *, t):
    x = x_ref[0]
    H, T = x.shape
    r = lax.broadcasted_iota(jnp.int32, (_CS, _CS), 0)
    c = lax.broadcasted_iota(jnp.int32, (_CS, _CS), 1)
    tri = (r <= c).astype(BF16)
    carry = jnp.zeros((H, 1), F32)
    chunks = []
    for ch in range(T // _CS):
        xh, xm, xl = _split3(x[:, ch * _CS:(ch + 1) * _CS])
        y = _dot(xh, tri) + _dot(xm, tri) + _dot(xl, tri) + carry
        carry = y[:, _CS - 1:_CS]
        chunks.append(y)
    cs = jnp.concatenate(chunks, axis=1) * LOG2E
    hi, mid, lo = _split3(cs)
    parts = jnp.concatenate([hi.astype(F32), mid.astype(F32), lo.astype(F32), jnp.zeros((H, T), F32)], axis=0)

    n_q = H * Q_AUG_FOX
    row = lax.broadcasted_iota(jnp.int32, (n_q, 4 * H), 0)
    src = lax.broadcasted_iota(jnp.int32, (n_q, 4 * H), 1)
    sh = Q_AUG_FOX.bit_length() - 1
    place_q = ((src == (row & (Q_AUG_FOX - 1)) * H + jnp.right_shift(row, sh)) & ((row & (Q_AUG_FOX - 1)) < 3))
    rq = lax.broadcasted_iota(jnp.int32, (n_q, T), 0) & (Q_AUG_FOX - 1)
    qa = _dot(place_q.astype(BF16), parts.astype(BF16)) + jnp.where((rq >= 3) & (rq < 6), 1.0, 0.0)
    qa = qa.astype(BF16)
    for h in range(H):
        for i in range(T // t):
            qa_ref[0, h, i] = qa[h * Q_AUG_FOX:(h + 1) * Q_AUG_FOX, i * t:(i + 1) * t]

    n_k = H * HEAD_DIM
    src = lax.broadcasted_iota(jnp.int32, (4 * H, n_k), 0)
    col = lax.broadcasted_iota(jnp.int32, (4 * H, n_k), 1)
    ck = col & (HEAD_DIM - 1)
    hk = jnp.right_shift(col, HEAD_DIM.bit_length() - 1)
    place_k = jnp.where((src == (ck - 3) * H + hk) & (ck >= 3) & (ck < 6), -1.0, 0.0).astype(BF16)
    ckt = lax.broadcasted_iota(jnp.int32, (T, n_k), 1) & (HEAD_DIM - 1)
    ka = _dot(parts.T.astype(BF16), place_k) + jnp.where(ckt < 3, 1.0, 0.0)
    ka_ref[0] = ka.astype(BF16)


def _decay(lf, t):
    B, H, T = lf.shape
    return pl.pallas_call(
        functools.partial(_decay_kernel, t=t),
        grid=(B,),
        in_specs=[pl.BlockSpec((1, H, T), lambda b: (b, 0, 0))],
        out_specs=[pl.BlockSpec((1, H, T // t, Q_AUG_FOX, t), lambda b: (b, 0, 0, 0, 0)),
                   pl.BlockSpec((1, T, H * HEAD_DIM), lambda b: (b, 0, 0))],
        out_shape=[jax.ShapeDtypeStruct((B, H, T // t, Q_AUG_FOX, t), BF16),
                   jax.ShapeDtypeStruct((B, T, H * HEAD_DIM), BF16)],
        compiler_params=_params(("parallel",)),
        name="decay_cumsum",
    )(lf)


def _flash_kernel(q_ref, k_ref, v_ref, mask_ref, g_ref, *rest, t, back, n_aug, n_extra):
    aug_ref = rest[0] if n_aug else None
    extras = rest[n_aug:n_aug + n_extra]
    o_ref, s_ref, p_ref = rest[n_aug + n_extra:]
    hb, nq, hd = q_ref.shape[1], q_ref.shape[2], q_ref.shape[3]
    rep = hb // v_ref.shape[1]
    n_pairs = sum(min(i, back) + 1 for i in range(nq))

    def masked_scores(h, i, j):
        rows = [q_ref[0, h, i]]
        if aug_ref is not None:
            rows.append(aug_ref[0, h if aug_ref.shape[1] == hb else h // rep, i])
        pad = AUG - sum(r.shape[0] for r in rows)
        q = jnp.concatenate(rows + [jnp.zeros((pad, t), BF16)], axis=0)
        kh = h // rep
        k = k_ref[0, pl.ds(pl.multiple_of(j * t, t), t), kh * AUG:(kh + 1) * AUG]
        s = _dot(k, q) + mask_ref[jnp.maximum(j - i + (mask_ref.shape[0] - 1), 0)]
        s_ref[h] = s
        return jnp.max(s, axis=0, keepdims=True)

    def weighted_values(h, j):
        return _dot(v_ref[0, h // rep, j], p_ref[h])

    def emit(h, i, acc):
        l = acc[hd:hd + 1]
        o_ref[0, h, i] = (acc[:hd] * jnp.where(l > 0.0, g_ref[0, h, i] / l, 0.0)).astype(o_ref.dtype)

    tile_max = []
    for h in range(hb):
        tile_max.append(masked_scores(h, 0, 0))
        p_ref[h] = jnp.zeros((t, t), BF16)

    def body(_, carry):
        (i, j, i_prev, j_prev), heads = carry
        row_end = j == i
        i_next = jnp.minimum(jnp.where(row_end, i + 1, i), nq - 1)
        j_next = jnp.where(row_end, jnp.maximum(i_next - back, 0), j + 1)
        row_start = j == jnp.maximum(i - back, 0)
        out = []
        for h in range(hb):
            m, acc, alpha, s_max = heads[h]
            acc = acc * alpha + weighted_values(h, j_prev)
            emit(h, i_prev, acc)
            m = jnp.where(row_start, MASKED, m)
            m_new = jnp.maximum(m, s_max)
            alpha = jnp.exp2(m - m_new)
            p_ref[h] = jnp.exp2(s_ref[h] - m_new).astype(BF16)
            s_max = masked_scores(h, i_next, j_next)
            out.append((m_new, acc, alpha, s_max))
        return (i_next, j_next, i, j), tuple(out)

    zero = jnp.int32(0)
    heads = tuple((jnp.full((1, t), MASKED, F32), jnp.zeros((v_ref.shape[3], t), F32),
                   jnp.ones((1, t), F32), tile_max[h]) for h in range(hb))
    _, heads = lax.fori_loop(0, n_pairs, body, ((zero, zero, zero, zero), heads))
    for h in range(hb):
        m, acc, alpha, _ = heads[h]
        emit(h, nq - 1, acc * alpha + weighted_values(h, nq - 1))
    for x_ref in extras:
        for h in range(hb):
            for i in range(nq):
                o_ref[0, h, i] = (o_ref[0, h, i].astype(F32) + x_ref[0, h, i].astype(F32)).astype(o_ref.dtype)


def _mask_table(t, n, window):
    k = jnp.arange(t)[:, None]
    q = jnp.arange(t)[None, :]
    tiles = []
    for d in range(n):
        off = (d - (n - 1)) * t
        ok = (k + off) <= q
        if window is not None:
            ok = ok & ((k + off) > q - window)
        tiles.append(ok)
    return jnp.where(jnp.stack(tiles), 0.0, MASKED).astype(F32)


def _flash(q, ka, v_t, gate, aug=None, extras=(), *, hb, window=None):
    B, H, nq, hd, t = q.shape
    G = v_t.shape[1]
    T = nq * t
    assert hb % (H // G) == 0
    kb = hb // (H // G)
    if window is None:
        back = nq - 1
        masks = _mask_table(t, 2, None)
    else:
        assert window % t == 0
        back = window // t
        masks = _mask_table(t, back + 1, window)
    tile = lambda n, c: pl.BlockSpec((1, n, nq, c, t), lambda b, h: (b, h, 0, 0, 0))
    augs = () if aug is None else (aug,)
    aug_specs = [tile(hb if a.shape[1] == H else kb, a.shape[3]) for a in augs]
    return pl.pallas_call(
        functools.partial(_flash_kernel, t=t, back=back, n_aug=len(augs), n_extra=len(extras)),
        grid=(B, H // hb),
        in_specs=[
            tile(hb, hd),
            pl.BlockSpec((1, T, kb * AUG), lambda b, h: (b, 0, h)),
            tile(kb, v_t.shape[3]),
            pl.BlockSpec(masks.shape, lambda b, h: (0, 0, 0)),
            tile(hb, 1),
        ] + aug_specs + [tile(hb, hd)] * len(extras),
        out_specs=tile(hb, hd),
        out_shape=jax.ShapeDtypeStruct((B, H, nq, hd, t), BF16),
        scratch_shapes=[pltpu.VMEM((hb, t, t), F32), pltpu.VMEM((hb, t, t), BF16)],
        compiler_params=_params(("parallel", "parallel")),
        name="flash_window" if window is not None else "flash_causal",
    )(q, ka, v_t, masks, gate, *augs, *extras)


def _compress_kernel(x_ref, w1_ref, pe_ref, w1p_ref, w2k_ref, w2vt_ref, kc_ref, vct_ref):
    half = CMP_STRIDE * HEAD_DIM
    n_chunk = x_ref.shape[1]
    x = x_ref[0]
    for kind in range(2):
        w1p = w1p_ref[kind]
        pe = pe_ref[kind].astype(BF16)
        bias = (_dot(pe[:, :half], w1p[:, :CMP_HIDDEN]) + _dot(pe[:, half:], w1p[:, CMP_HIDDEN:]))[0:1]
        ab = _dot(x, w1_ref[kind])
        for g in range(NSA_KV_HEADS):
            lo = g * 2 * CMP_HIDDEN
            nxt = pltpu.roll(ab[:, lo + CMP_HIDDEN:lo + 2 * CMP_HIDDEN], n_chunk - 1, 0)
            pre = ab[:, lo:lo + CMP_HIDDEN] + nxt + bias
            hdn = (pre * jax.nn.sigmoid(pre)).astype(BF16)
            if kind == 0:
                kc_ref[0, g] = _dot(hdn, w2k_ref[...]).astype(BF16)
            else:
                vct_ref[0, g] = _dot_nt(w2vt_ref[...], hdn).astype(BF16)


def _compress(xc, w1exp, pe8, w1cat, w2k, w2vt):
    B, n_chunk, _ = xc.shape
    full = lambda a: pl.BlockSpec(a.shape, lambda b: (0,) * a.ndim)
    return pl.pallas_call(
        _compress_kernel,
        grid=(B,),
        in_specs=[pl.BlockSpec((1,) + xc.shape[1:], lambda b: (b, 0, 0)),
                  full(w1exp), full(pe8), full(w1cat), full(w2k), full(w2vt)],
        out_specs=[pl.BlockSpec((1, NSA_KV_HEADS, n_chunk, HEAD_DIM), lambda b: (b, 0, 0, 0)),
                   pl.BlockSpec((1, NSA_KV_HEADS, HEAD_DIM, n_chunk), lambda b: (b, 0, 0, 0))],
        out_shape=[jax.ShapeDtypeStruct((B, NSA_KV_HEADS, n_chunk, HEAD_DIM), BF16),
                   jax.ShapeDtypeStruct((B, NSA_KV_HEADS, HEAD_DIM, n_chunk), BF16)],
        compiler_params=_params(("parallel",)),
        name="nsa_compress",
    )(xc, w1exp, pe8, w1cat, w2k, w2vt)


def _cmp_attn_kernel(q_ref, kc_ref, vct_ref, g_ref, o_ref, sb_ref, *, tq, t):
    i = pl.program_id(2)
    n_cmp = kc_ref.shape[2]
    n_sel = sb_ref.shape[3]
    n_io = lax.broadcasted_iota(jnp.int32, (n_cmp, tq), 0)
    t_io = i * tq + lax.broadcasted_iota(jnp.int32, (n_cmp, tq), 1)
    valid = n_io * CMP_STRIDE + (CMP_LEN - 1) <= t_io
    kc = kc_ref[0, 0]
    vct = vct_ref[0, 0]
    psum = jnp.zeros((n_cmp, tq), F32)
    for r in range(NSA_REP):
        q = q_ref[0, r * HEAD_DIM:(r + 1) * HEAD_DIM, :]
        s = jnp.where(valid, _dot(kc, q), MASKED)
        m = jnp.max(s, axis=0, keepdims=True)
        e = jnp.where(valid, jnp.exp2(s - m), 0.0)
        l = jnp.sum(e, axis=0, keepdims=True)
        p = e * jnp.where(l > 0.0, 1.0 / l, 0.0)
        psum = psum + p
        o = _dot(vct, p.astype(BF16))
        o = (o * g_ref[0, 0, 0, r:r + 1, :]).astype(BF16)
        for c in range(tq // t):
            o_ref[0, r, c] = o[:, c * t:(c + 1) * t]

    jr = lax.broadcasted_iota(jnp.int32, (n_sel, n_cmp), 0)
    nc = lax.broadcasted_iota(jnp.int32, (n_sel, n_cmp), 1)
    overlap = ((nc * CMP_STRIDE < (jr + 1) * SEL_BLOCK)
               & (nc * CMP_STRIDE + CMP_LEN > jr * SEL_BLOCK)).astype(BF16)
    ph, pm, pl_ = _split3(psum)
    imp = _dot(overlap, ph) + _dot(overlap, pm) + _dot(overlap, pl_)
    j_io = lax.broadcasted_iota(jnp.int32, (n_sel, tq), 0)
    cur = jnp.right_shift(i * tq + lax.broadcasted_iota(jnp.int32, (n_sel, tq), 1),
                          SEL_BLOCK.bit_length() - 1)
    is_cur = j_io == cur
    is_fixed = (j_io == 0) | (j_io == cur - 1)
    imp = jnp.where(is_cur, 2.0 * FORCE_SCORE, jnp.where(is_fixed, FORCE_SCORE, imp))
    imp = jnp.where(j_io <= cur, imp, -1.0)
    grp = 8
    imp_g = [imp[k * grp:(k + 1) * grp] for k in range(n_sel // grp)]
    sub = lax.broadcasted_iota(jnp.int32, (grp, tq), 0)
    cnt_g = [jnp.zeros((grp, tq), F32) for _ in imp_g]
    for jp in range(n_sel):
        row = imp[jp:jp + 1, :]
        for k, x in enumerate(imp_g):
            ge = jnp.where(row >= x, 1.0, 0.0)
            gt = jnp.where(row > x, 1.0, 0.0)
            if k > jp // grp:
                beats = ge
            elif k < jp // grp:
                beats = gt
            else:
                beats = jnp.where(sub > jp % grp, ge, gt)
            cnt_g[k] = cnt_g[k] + beats
    cnt = jnp.concatenate(cnt_g, axis=0)
    sb = jnp.where(cnt < min(SEL_TOPK, n_sel), 0.0, MASKED).astype(BF16)
    for c in range(tq // t):
        sb_ref[0, 0, c] = sb[:, c * t:(c + 1) * t]


def _cmp_attn(nq_t, kc, vct, gate_cmp, n_sel, t, tq=1024):
    B, _, T = nq_t.shape
    G = kc.shape[1]
    rows = NSA_REP * HEAD_DIM
    return pl.pallas_call(
        functools.partial(_cmp_attn_kernel, tq=tq, t=t),
        grid=(B, G, T // tq),
        in_specs=[
            pl.BlockSpec((1, rows, tq), lambda b, g, i: (b, g, i)),
            pl.BlockSpec((1, 1) + kc.shape[2:], lambda b, g, i: (b, g, 0, 0)),
            pl.BlockSpec((1, 1) + vct.shape[2:], lambda b, g, i: (b, g, 0, 0)),
            pl.BlockSpec((1, 1, 1, NSA_REP, tq), lambda b, g, i: (b, 0, g, 0, i)),
        ],
        out_specs=[pl.BlockSpec((1, NSA_REP, tq // t, HEAD_DIM, t), lambda b, g, i: (b, g, i, 0, 0)),
                   pl.BlockSpec((1, 1, tq // t, n_sel, t), lambda b, g, i: (b, g, i, 0, 0))],
        out_shape=[jax.ShapeDtypeStruct((B, G * NSA_REP, T // t, HEAD_DIM, t), BF16),
                   jax.ShapeDtypeStruct((B, G, T // t, n_sel, t), BF16)],
        compiler_params=_params(("parallel", "parallel", "parallel")),
        name="nsa_cmp_attn",
    )(nq_t, kc, vct, gate_cmp)


def _mix_out_kernel(x_ref, of_ref, on_ref, wot_ref, g_ref, o_ref):
    heads, n_tiles, hd, t = of_ref.shape[1:]
    for c in range(n_tiles):
        of = of_ref[0, :, c].reshape(heads * hd, t)
        on = on_ref[0, :, c].reshape(heads * hd, t)
        mix_t = _dot(wot_ref[:, 0:FOX_QKV], of) + _dot(wot_ref[:, FOX_QKV:], on)
        rows = slice(c * t, (c + 1) * t)
        o_ref[0, rows, :] = x_ref[0, rows, :] + _rms(mix_t.T, g_ref[...])


def _mix_out(x, o_fox, o_nsa, wot, g, tm=512):
    B, T, D = x.shape
    heads, _, hd, t = o_fox.shape[1:]
    row = pl.BlockSpec((1, tm, D), lambda b, i: (b, i, 0))
    tiles = pl.BlockSpec((1, heads, tm // t, hd, t), lambda b, i: (b, 0, i, 0, 0))
    return pl.pallas_call(
        _mix_out_kernel,
        grid=(B, T // tm),
        in_specs=[row, tiles, tiles,
                  pl.BlockSpec(wot.shape, lambda b, i: (0, 0)),
                  pl.BlockSpec((1, D), lambda b, i: (0, 0))],
        out_specs=row,
        out_shape=jax.ShapeDtypeStruct((B, T, D), F32),
        compiler_params=_params(("parallel", "parallel")),
        name="mix_out",
    )(x, o_fox, o_nsa, wot, g)


def _mem_kv_kernel(m_ref, g_ref, w_ref, k_ref, v_ref):
    m = _rms(m_ref[0], g_ref[...]).astype(BF16)
    k_ref[0] = _dot(m, w_ref[:, :D_MODEL]).astype(BF16)
    v_ref[0] = _dot(m, w_ref[:, D_MODEL:]).astype(BF16)


def _mem_kv(mem, g, wkv):
    B, M, D = mem.shape
    blk = pl.BlockSpec((1, M, D), lambda b: (b, 0, 0))
    return pl.pallas_call(
        _mem_kv_kernel,
        grid=(B,),
        in_specs=[blk, pl.BlockSpec((1, D), lambda b: (0, 0)), pl.BlockSpec(wkv.shape, lambda b: (0, 0))],
        out_specs=[blk, blk],
        out_shape=[jax.ShapeDtypeStruct((B, M, D), BF16)] * 2,
        compiler_params=_params(("parallel",)),
        name="mem_kv",
    )(mem, g, wkv)


def _cross_kernel(h_ref, k_ref, v_ref, wq_ref, wo_ref, gpre_ref, gpost_ref, o_ref):
    h = h_ref[0]
    n = _rms(h, gpre_ref[...]).astype(BF16)
    scale = CROSS_HEAD_DIM ** -0.5
    q_all = (_dot(n, wq_ref[...]) * scale).astype(BF16)
    heads = []
    for hh in range(CROSS_HEADS):
        sl = slice(hh * CROSS_HEAD_DIM, (hh + 1) * CROSS_HEAD_DIM)
        s = _dot_nt(q_all[:, sl], k_ref[0, :, sl])
        m = jnp.max(s, axis=-1, keepdims=True)
        p = jnp.exp(s - m)
        l = jnp.sum(p, axis=-1, keepdims=True)
        heads.append((_dot(p.astype(BF16), v_ref[0, :, sl]) / l).astype(BF16))
    out = _dot(jnp.concatenate(heads, axis=-1), wo_ref[...])
    o_ref[0] = h + _rms(out, gpost_ref[...])


def _cross(h, k, v, wq, wo, g_pre, g_post, tm=512):
    B, T, D = h.shape
    M = k.shape[1]
    row = pl.BlockSpec((1, tm, D), lambda b, i: (b, i, 0))
    kvb = pl.BlockSpec((1, M, D), lambda b, i: (b, 0, 0))
    wsp = pl.BlockSpec((D, D), lambda b, i: (0, 0))
    gsp = pl.BlockSpec((1, D), lambda b, i: (0, 0))
    return pl.pallas_call(
        _cross_kernel,
        grid=(B, T // tm),
        in_specs=[row, kvb, kvb, wsp, wsp, gsp, gsp],
        out_specs=row,
        out_shape=jax.ShapeDtypeStruct((B, T, D), F32),
        compiler_params=_params(("parallel", "parallel")),
        name="mem_cross",
    )(h, k, v, wq, wo, g_pre, g_post)


def _mlp_kernel(h_ref, wu_ref, wd_ref, gpre_ref, gpost_ref, o_ref, *, hc):
    h = h_ref[...]
    n = _rms(h, gpre_ref[...]).astype(BF16)
    acc = jnp.zeros(h.shape, F32)
    for c in range(wu_ref.shape[1] // hc):
        u = jnp.maximum(_dot(n, wu_ref[:, c * hc:(c + 1) * hc]), 0.0)
        acc = acc + _dot((u * u).astype(BF16), wd_ref[c * hc:(c + 1) * hc, :])
    o_ref[...] = h + _rms(acc, gpost_ref[...])


def _mlp(h, wu, wd, g_pre, g_post, tm=512, hc=512):
    N, D = h.shape
    row = pl.BlockSpec((tm, D), lambda i: (i, 0))
    gsp = pl.BlockSpec((1, D), lambda i: (0, 0))
    once = pl.Buffered(1)
    return pl.pallas_call(
        functools.partial(_mlp_kernel, hc=hc),
        grid=(N // tm,),
        in_specs=[row,
                  pl.BlockSpec(wu.shape, lambda i: (0, 0), pipeline_mode=once),
                  pl.BlockSpec(wd.shape, lambda i: (0, 0), pipeline_mode=once),
                  gsp, gsp],
        out_specs=row,
        out_shape=jax.ShapeDtypeStruct((N, D), F32),
        compiler_params=_params(("parallel",)),
        name="relu2_mlp",
    )(h, wu, wd, g_pre, g_post)


def _layer(h, mem, g_mix_pre, w_in, b_forget, w_ck1, w_ck2, w_cv1, w_cv2, pe_k, pe_v,
           w_mix_out, g_mix_post, g_x_pre, g_mem, w_xq, w_xkv, w_xo, g_x_post,
           g_mlp_pre, w_up, w_down, g_mlp_post):
    B, T, D = h.shape
    H, G, hd = NSA_HEADS, NSA_KV_HEADS, HEAD_DIM
    row = lambda g: g.reshape(1, -1)

    cols = {}
    lo = 0
    for name, size in (("fq", FOX_QKV), ("fk", FOX_QKV), ("fv", FOX_QKV), ("ff", FOX_HEADS), ("nq", NSA_Q),
                       ("kc", NSA_KV), ("vc", NSA_KV), ("ks", NSA_KV), ("vs", NSA_KV), ("kw", NSA_KV),
                       ("vw", NSA_KV), ("ng", NSA_HEADS * N_BRANCH)):
        cols[name] = w_in[:, lo:lo + size]
        lo += size
    ng_branch_major = cols["ng"].reshape(D, NSA_HEADS, N_BRANCH).transpose(0, 2, 1).reshape(D, -1)
    wt = jnp.concatenate([cols[k] for k in ("fq", "fv", "nq", "ks", "vs", "kw", "vw", "ff")]
                         + [ng_branch_major], axis=1).T.astype(BF16)
    wk = jnp.concatenate([cols["fk"], cols["kc"], cols["vc"]], axis=1).astype(BF16)
    half = hd // 2
    inv = ROPE_THETA ** (-jnp.arange(half, dtype=F32) / half)
    ang = inv[:, None] * jnp.arange(T, dtype=F32)[None, :]
    cos_t, sin_t = jnp.cos(ang), jnp.sin(ang)

    t = 256
    hb = 8
    nq = T // t
    (fq, fk, fv, nq_t, nqr, kcvc, ks_t, vs, kw_t, vw, lf_t, gt_t) = _in_proj(
        h, row(g_mix_pre), wt, wk, b_forget.reshape(FOX_HEADS, 1), cos_t, sin_t, t)

    def with_aug(k_tok, aug_tok, heads):
        return jnp.concatenate([k_tok.reshape(B, T, heads, hd), aug_tok.reshape(B, T, heads, AUG - hd)],
                               axis=3).reshape(B, T, heads * AUG)

    def tile_major(a):
        return a.reshape(a.shape[:3] + (nq, t)).transpose(0, 1, 3, 2, 4)

    aug_fox, kaug_fox = _decay(lf_t, t)
    one_gate = jnp.ones((B, FOX_HEADS, nq, 1, t), F32)
    o_fox = _flash(fq, with_aug(fk, kaug_fox, FOX_HEADS), fv, one_gate, aug_fox, hb=hb)

    n_chunk = T // CMP_STRIDE
    hsz = CMP_STRIDE * hd
    w1cat = jnp.stack([jnp.concatenate([w[:hsz], w[hsz:]], axis=1) for w in (w_ck1, w_cv1)]).astype(BF16)
    w1exp = jnp.zeros((2, CMP_STRIDE, 2 * G, hd, G, 2, CMP_HIDDEN), F32)
    for kind, w in enumerate((w_ck1, w_cv1)):
        halves = w.reshape(2, CMP_STRIDE, hd, CMP_HIDDEN).transpose(1, 2, 0, 3)
        for g in range(G):
            w1exp = w1exp.at[kind, :, kind * G + g, :, g].set(halves)
    w1exp = w1exp.reshape(2, CMP_STRIDE * 2 * G * hd, G * 2 * CMP_HIDDEN).astype(BF16)
    pe8 = jnp.stack([jnp.broadcast_to(p.reshape(1, -1), (8, CMP_LEN * hd)) for p in (pe_k, pe_v)])
    kc, vct = _compress(kcvc.reshape(B, n_chunk, CMP_STRIDE * 2 * G * hd), w1exp, pe8, w1cat,
                        w_ck2.astype(BF16), w_cv2.T.astype(BF16))

    gates = gt_t.reshape(B, N_BRANCH, G, NSA_REP, T)
    n_sel = T // SEL_BLOCK
    o_cmp, selbias = _cmp_attn(nq_t, kc, vct, gates[:, 0:1], n_sel, t)

    onehot = (jnp.arange(T)[:, None] // SEL_BLOCK == jnp.arange(n_sel)[None, :]).astype(BF16)
    aug_slc = jnp.concatenate([onehot, jnp.zeros((T, AUG - hd - n_sel), BF16)], axis=1)
    ka_slc = with_aug(ks_t.transpose(0, 2, 1), jnp.broadcast_to(aug_slc[None, :, None, :], (B, T, G, AUG - hd)), G)
    gate_h = tile_major(gt_t.reshape(B * N_BRANCH, H, 1, T)).reshape(B, N_BRANCH, H, nq, 1, t)
    o_slc = _flash(nqr, ka_slc, vs, gate_h[:, 1], selbias, hb=hb)

    ka_win = with_aug(kw_t.transpose(0, 2, 1), jnp.zeros((B, T, G, AUG - hd), BF16), G)
    o_nsa = _flash(nqr, ka_win, vw, gate_h[:, 2], extras=(o_cmp, o_slc), hb=hb, window=WINDOW)

    h = _mix_out(h, o_fox, o_nsa, w_mix_out.T.astype(BF16), row(g_mix_post))

    k_mem, v_mem = _mem_kv(mem, row(g_mem), w_xkv.astype(BF16))
    h = _cross(h, k_mem, v_mem, w_xq.astype(BF16), w_xo.astype(BF16), row(g_x_pre), row(g_x_post))

    h = _mlp(h.reshape(B * T, D), w_up.astype(BF16), w_down.astype(BF16),
             row(g_mlp_pre), row(g_mlp_post)).reshape(B, T, D)
    return h


def kernel(x, mem, g_mix_pre, w_in, b_forget, w_ck1, w_ck2, w_cv1, w_cv2, pe_k, pe_v, w_mix_out, g_mix_post,
           g_x_pre, g_mem, w_xq, w_xkv, w_xo, g_x_post, g_mlp_pre, w_up, w_down, g_mlp_post):
    h = x
    for l in range(g_mix_pre.shape[0]):
        h = _layer(h, mem, g_mix_pre[l], w_in[l], b_forget[l], w_ck1[l], w_ck2[l], w_cv1[l], w_cv2[l],
                   pe_k[l], pe_v[l], w_mix_out[l], g_mix_post[l], g_x_pre[l], g_mem[l], w_xq[l], w_xkv[l],
                   w_xo[l], g_x_post[l], g_mlp_pre[l], w_up[l], w_down[l], g_mlp_post[l])
    return h
```

```python
import functools
import math

import jax
import jax.numpy as jnp
from jax import lax
from jax.experimental import pallas as pl
from jax.experimental.pallas import tpu as pltpu

D_MODEL = 1024
HEAD_DIM = 64
FOX_HEADS = 8
NSA_HEADS = 8
NSA_KV_HEADS = 2
NSA_REP = NSA_HEADS // NSA_KV_HEADS
CMP_LEN = 32
CMP_STRIDE = 16
CMP_HIDDEN = 2 * HEAD_DIM
SEL_BLOCK = 64
SEL_TOPK = 16
WINDOW = 512
N_BRANCH = 3
CROSS_HEADS = 4
CROSS_HEAD_DIM = D_MODEL // CROSS_HEADS
MLP_HIDDEN = 4 * D_MODEL
ROPE_THETA = 10000.0
RMS_EPS = 1e-6
FORCE_SCORE = 1e4
MASKED = -1e30
LOG2E = math.log2(math.e)

FOX_QKV = FOX_HEADS * HEAD_DIM
NSA_Q = NSA_HEADS * HEAD_DIM
NSA_KV = NSA_KV_HEADS * HEAD_DIM
AUG = 128

V7X_VMEM_LIMIT = 56 * 1024 * 1024

F32 = jnp.float32
BF16 = jnp.bfloat16


def _params(sem, vmem=V7X_VMEM_LIMIT):
    return pltpu.CompilerParams(dimension_semantics=sem, vmem_limit_bytes=vmem)


def _rms(x, g):
    return x * lax.rsqrt(jnp.mean(x * x, axis=-1, keepdims=True) + RMS_EPS) * g


def _dot(a, b):
    return jnp.dot(a, b, preferred_element_type=F32)


def _dot_nt(a, b):
    return lax.dot_general(a, b, (((1,), (1,)), ((), ())), preferred_element_type=F32)


def _split3(x):
    hi = x.astype(BF16)
    r1 = x - hi.astype(F32)
    mid = r1.astype(BF16)
    lo = (r1 - mid.astype(F32)).astype(BF16)
    return hi, mid, lo


_R_FQ, _R_FV, _R_NQ, _R_VS, _R_VW, _R_FF, _R_NG, _R_END = 0, 512, 1024, 1536, 1664, 1792, 1800, 1824
_C_FK, _C_KC, _C_KS, _C_KW, _C_END = 0, 512, 768, 896, 1024
V_ROWS = HEAD_DIM + 16


def _in_proj_kernel(x_ref, g_ref, wt_ref, wk_ref, bf_ref, cos_ref, sin_ref, cosk_ref, sink_ref, pfox_ref, prope_ref,
                    fq_ref, kfox_ref, fv_ref, nq_ref, nqr_ref, kcvc_ref,
                    kslc_ref, vs_ref, kwin_ref, vw_ref, lf_ref, gt_ref, gtt_ref, *, t):
    n = _rms(x_ref[0], g_ref[...]).astype(BF16)
    tm = n.shape[0]
    cos = cos_ref[...]
    sin = sin_ref[...]
    half = HEAD_DIM // 2

    def proj(lo, hi):
        return _dot_nt(wt_ref[lo:hi, :], n)

    def rope(r):
        parts = []
        for h in range(r.shape[0] // HEAD_DIM):
            x1 = r[h * HEAD_DIM:h * HEAD_DIM + half]
            x2 = r[h * HEAD_DIM + half:(h + 1) * HEAD_DIM]
            parts += [x1 * cos - x2 * sin, x2 * cos + x1 * sin]
        return jnp.concatenate(parts, axis=0)

    def store_tiles(r, out_ref):
        r = r.astype(BF16)
        rows = out_ref.shape[3]
        if rows > HEAD_DIM:
            sub = lax.broadcasted_iota(jnp.int32, (rows - HEAD_DIM, t), 0)
            tail = jnp.where(sub == 0, 1.0, 0.0).astype(BF16)
        for h in range(r.shape[0] // HEAD_DIM):
            for c in range(tm // t):
                tile = r[h * HEAD_DIM:(h + 1) * HEAD_DIM, c * t:(c + 1) * t]
                out_ref[0, h, c] = tile if rows == HEAD_DIM else jnp.concatenate([tile, tail], axis=0)

    scale = HEAD_DIM ** -0.5 * LOG2E
    store_tiles(proj(_R_FQ, _R_FV) * scale, fq_ref)
    store_tiles(proj(_R_FV, _R_NQ), fv_ref)
    nq = proj(_R_NQ, _R_VS) * scale
    nq_ref[0] = nq.astype(BF16)
    store_tiles(rope(nq), nqr_ref)
    store_tiles(proj(_R_VS, _R_VW), vs_ref)
    store_tiles(proj(_R_VW, _R_FF), vw_ref)
    small = proj(_R_FF, _R_END)
    z = small[0:FOX_HEADS] + bf_ref[...]
    lf_ref[0] = jnp.minimum(z, 0.0) - jnp.log1p(jnp.exp(-jnp.abs(z)))
    gates = jax.nn.sigmoid(small[FOX_HEADS:])
    gt_ref[0] = gates
    for r in range(gates.shape[0]):
        for c in range(tm // t):
            gtt_ref[0, r, c] = gates[r:r + 1, c * t:(c + 1) * t]

    tok = _dot(n, wk_ref[...])
    kfox_ref[0] = _dot(tok[:, _C_FK:_C_KC].astype(BF16), pfox_ref[...]).astype(BF16)
    kcvc_ref[0, 0] = tok[:, _C_KC:_C_KC + NSA_KV]
    kcvc_ref[0, 1] = tok[:, _C_KC + NSA_KV:_C_KS]

    def rope_placed(k):
        both = jnp.concatenate([(k * cosk_ref[...]).astype(BF16), (k * sink_ref[...]).astype(BF16)], axis=1)
        return _dot(both, prope_ref[...])

    lane = lax.broadcasted_iota(jnp.int32, (tm, NSA_KV_HEADS * AUG), 1) & (AUG - 1)
    pos = pl.program_id(1) * tm + lax.broadcasted_iota(jnp.int32, (tm, NSA_KV_HEADS * AUG), 0)
    block_col = lane == HEAD_DIM + jnp.right_shift(pos, SEL_BLOCK.bit_length() - 1)
    kslc_ref[0] = (rope_placed(tok[:, _C_KS:_C_KW]) + jnp.where(block_col, 1.0, 0.0)).astype(BF16)
    kwin_ref[0] = rope_placed(tok[:, _C_KW:_C_END]).astype(BF16)


def _in_proj(x, g, wt, wk, b_forget, cos_t, sin_t, cos_k, sin_k, place_fox, place_rope, t, tm=512):
    B, T, D = x.shape
    grid = (B, T // tm)
    chan = lambda c, dt: (jax.ShapeDtypeStruct((B, c, T), dt), pl.BlockSpec((1, c, tm), lambda b, i: (b, 0, i)))
    toks = lambda c: (jax.ShapeDtypeStruct((B, T, c), BF16), pl.BlockSpec((1, tm, c), lambda b, i: (b, i, 0)))
    tiles = lambda h, rows, dt=BF16: (jax.ShapeDtypeStruct((B, h, T // t, rows, t), dt),
                                      pl.BlockSpec((1, h, tm // t, rows, t), lambda b, i: (b, 0, i, 0, 0)))
    n_gate = NSA_HEADS * N_BRANCH
    outs = [tiles(FOX_HEADS, HEAD_DIM), toks(FOX_HEADS * AUG), tiles(FOX_HEADS, V_ROWS), chan(NSA_Q, BF16),
            tiles(NSA_HEADS, HEAD_DIM),
            (jax.ShapeDtypeStruct((B, 2, T, NSA_KV), F32), pl.BlockSpec((1, 2, tm, NSA_KV), lambda b, i: (b, 0, i, 0))),
            toks(NSA_KV_HEADS * AUG), tiles(NSA_KV_HEADS, V_ROWS), toks(NSA_KV_HEADS * AUG),
            tiles(NSA_KV_HEADS, V_ROWS), chan(FOX_HEADS, F32), chan(n_gate, F32), tiles(n_gate, 1, F32)]
    full = lambda a: pl.BlockSpec(a.shape, lambda b, i: (0,) * a.ndim)
    return pl.pallas_call(
        functools.partial(_in_proj_kernel, t=t),
        grid=grid,
        in_specs=[
            pl.BlockSpec((1, tm, D), lambda b, i: (b, i, 0)),
            full(g), full(wt), full(wk), full(b_forget),
            pl.BlockSpec((HEAD_DIM // 2, tm), lambda b, i: (0, i)),
            pl.BlockSpec((HEAD_DIM // 2, tm), lambda b, i: (0, i)),
            pl.BlockSpec((tm, NSA_KV), lambda b, i: (i, 0)),
            pl.BlockSpec((tm, NSA_KV), lambda b, i: (i, 0)),
            full(place_fox), full(place_rope),
        ],
        out_specs=[o[1] for o in outs],
        out_shape=[o[0] for o in outs],
        compiler_params=_params(("parallel", "parallel")),
        name="in_proj",
    )(x, g, wt, wk, b_forget, cos_t, sin_t, cos_k, sin_k, place_fox, place_rope)


_CS = 128
Q_AUG_FOX = 16


def _decay_kernel(x_ref, k_ref, qa_ref, ka_ref, *, t):
    x = x_ref[0]
    H, T = x.shape
    r = lax.broadcasted_iota(jnp.int32, (_CS, _CS), 0)
    c = lax.broadcasted_iota(jnp.int32, (_CS, _CS), 1)
    tri = (r <= c).astype(BF16)
    carry = jnp.zeros((H, 1), F32)
    chunks = []
    for ch in range(T // _CS):
        xh, xm, xl = _split3(x[:, ch * _CS:(ch + 1) * _CS])
        y = _dot(xh, tri) + _dot(xm, tri) + _dot(xl, tri) + carry
        carry = y[:, _CS - 1:_CS]
        chunks.append(y)
    cs = jnp.concatenate(chunks, axis=1) * LOG2E
    hi, mid, lo = _split3(cs)
    parts = jnp.concatenate([hi.astype(F32), mid.astype(F32), lo.astype(F32), jnp.zeros((H, T), F32)], axis=0)

    n_q = H * Q_AUG_FOX
    row = lax.broadcasted_iota(jnp.int32, (n_q, 4 * H), 0)
    src = lax.broadcasted_iota(jnp.int32, (n_q, 4 * H), 1)
    sh = Q_AUG_FOX.bit_length() - 1
    place_q = ((src == (row & (Q_AUG_FOX - 1)) * H + jnp.right_shift(row, sh)) & ((row & (Q_AUG_FOX - 1)) < 3))
    rq = lax.broadcasted_iota(jnp.int32, (n_q, T), 0) & (Q_AUG_FOX - 1)
    qa = _dot(place_q.astype(BF16), parts.astype(BF16)) + jnp.where((rq >= 3) & (rq < 6), 1.0, 0.0)
    qa = qa.astype(BF16)
    for h in range(H):
        for i in range(T // t):
            qa_ref[0, h, i] = qa[h * Q_AUG_FOX:(h + 1) * Q_AUG_FOX, i * t:(i + 1) * t]

    n_k = H * AUG
    src = lax.broadcasted_iota(jnp.int32, (4 * H, n_k), 0)
    col = lax.broadcasted_iota(jnp.int32, (4 * H, n_k), 1)
    ck = (col & (AUG - 1)) - HEAD_DIM
    hk = jnp.right_shift(col, AUG.bit_length() - 1)
    place_k = jnp.where((src == (ck - 3) * H + hk) & (ck >= 3) & (ck < 6), -1.0, 0.0).astype(BF16)
    ckt = (lax.broadcasted_iota(jnp.int32, (t, n_k), 1) & (AUG - 1)) - HEAD_DIM
    ones_k = jnp.where((ckt >= 0) & (ckt < 3), 1.0, 0.0)
    parts_t = parts.T.astype(BF16)
    for i in range(T // t):
        rows = slice(i * t, (i + 1) * t)
        ka_ref[0, rows, :] = (k_ref[0, rows, :].astype(F32) + _dot(parts_t[rows], place_k) + ones_k).astype(BF16)


def _decay(lf, k_placed, t):
    B, H, T = lf.shape
    kspec = pl.BlockSpec((1, T, H * AUG), lambda b: (b, 0, 0))
    return pl.pallas_call(
        functools.partial(_decay_kernel, t=t),
        grid=(B,),
        in_specs=[pl.BlockSpec((1, H, T), lambda b: (b, 0, 0)), kspec],
        out_specs=[pl.BlockSpec((1, H, T // t, Q_AUG_FOX, t), lambda b: (b, 0, 0, 0, 0)), kspec],
        out_shape=[jax.ShapeDtypeStruct((B, H, T // t, Q_AUG_FOX, t), BF16),
                   jax.ShapeDtypeStruct(k_placed.shape, BF16)],
        input_output_aliases={1: 1},
        compiler_params=_params(("parallel",)),
        name="decay_cumsum",
    )(lf, k_placed)


def _flash_kernel(q_ref, k_ref, v_ref, mask_ref, g_ref, *rest, t, back, n_aug, n_extra):
    aug_ref = rest[0] if n_aug else None
    extras = rest[n_aug:n_aug + n_extra]
    o_ref, s_ref, p_ref = rest[n_aug + n_extra:]
    hb, nq, hd = q_ref.shape[1], q_ref.shape[2], q_ref.shape[3]
    rep = hb // v_ref.shape[1]
    n_pairs = sum(min(i, back) + 1 for i in range(nq))

    def masked_scores(h, i, j):
        rows = [q_ref[0, h, i]]
        if aug_ref is not None:
            rows.append(aug_ref[0, h if aug_ref.shape[1] == hb else h // rep, i])
        pad = AUG - sum(r.shape[0] for r in rows)
        q = jnp.concatenate(rows + [jnp.zeros((pad, t), BF16)], axis=0)
        kh = h // rep
        k = k_ref[0, pl.ds(pl.multiple_of(j * t, t), t), kh * AUG:(kh + 1) * AUG]
        s = _dot(k, q) + mask_ref[jnp.maximum(j - i + (mask_ref.shape[0] - 1), 0)]
        s_ref[h] = s
        return jnp.max(s, axis=0, keepdims=True)

    def weighted_values(h, j):
        return _dot(v_ref[0, h // rep, j], p_ref[h])

    def emit(h, i, acc):
        l = acc[hd:hd + 1]
        o_ref[0, h, i] = (acc[:hd] * jnp.where(l > 0.0, g_ref[0, h, i] / l, 0.0)).astype(o_ref.dtype)

    tile_max = []
    for h in range(hb):
        tile_max.append(masked_scores(h, 0, 0))
        p_ref[h] = jnp.zeros((t, t), BF16)

    def body(_, carry):
        (i, j, i_prev, j_prev), heads = carry
        row_end = j == i
        i_next = jnp.minimum(jnp.where(row_end, i + 1, i), nq - 1)
        j_next = jnp.where(row_end, jnp.maximum(i_next - back, 0), j + 1)
        row_start = j == jnp.maximum(i - back, 0)
        out = []
        for h in range(hb):
            m, acc, alpha, s_max = heads[h]
            acc = acc * alpha + weighted_values(h, j_prev)
            emit(h, i_prev, acc)
            m = jnp.where(row_start, MASKED, m)
            m_new = jnp.maximum(m, s_max)
            alpha = jnp.exp2(m - m_new)
            p_ref[h] = jnp.exp2(s_ref[h] - m_new).astype(BF16)
            s_max = masked_scores(h, i_next, j_next)
            out.append((m_new, acc, alpha, s_max))
        return (i_next, j_next, i, j), tuple(out)

    zero = jnp.int32(0)
    heads = tuple((jnp.full((1, t), MASKED, F32), jnp.zeros((v_ref.shape[3], t), F32),
                   jnp.ones((1, t), F32), tile_max[h]) for h in range(hb))
    _, heads = lax.fori_loop(0, n_pairs, body, ((zero, zero, zero, zero), heads))
    for h in range(hb):
        m, acc, alpha, _ = heads[h]
        emit(h, nq - 1, acc * alpha + weighted_values(h, nq - 1))
    for x_ref in extras:
        for h in range(hb):
            for i in range(nq):
                o_ref[0, h, i] = (o_ref[0, h, i].astype(F32) + x_ref[0, h, i].astype(F32)).astype(o_ref.dtype)


def _mask_table(t, n, window):
    k = jnp.arange(t)[:, None]
    q = jnp.arange(t)[None, :]
    tiles = []
    for d in range(n):
        off = (d - (n - 1)) * t
        ok = (k + off) <= q
        if window is not None:
            ok = ok & ((k + off) > q - window)
        tiles.append(ok)
    return jnp.where(jnp.stack(tiles), 0.0, MASKED).astype(F32)


def _flash(q, ka, v_t, gate, aug=None, extras=(), *, hb, window=None):
    B, H, nq, hd, t = q.shape
    G = v_t.shape[1]
    T = nq * t
    assert hb % (H // G) == 0
    kb = hb // (H // G)
    if window is None:
        back = nq - 1
        masks = _mask_table(t, 2, None)
    else:
        assert window % t == 0
        back = window // t
        masks = _mask_table(t, back + 1, window)
    tile = lambda n, c: pl.BlockSpec((1, n, nq, c, t), lambda b, h: (b, h, 0, 0, 0))
    augs = () if aug is None else (aug,)
    aug_specs = [tile(hb if a.shape[1] == H else kb, a.shape[3]) for a in augs]
    return pl.pallas_call(
        functools.partial(_flash_kernel, t=t, back=back, n_aug=len(augs), n_extra=len(extras)),
        grid=(B, H // hb),
        in_specs=[
            tile(hb, hd),
            pl.BlockSpec((1, T, kb * AUG), lambda b, h: (b, 0, h)),
            tile(kb, v_t.shape[3]),
            pl.BlockSpec(masks.shape, lambda b, h: (0, 0, 0)),
            tile(hb, 1),
        ] + aug_specs + [tile(hb, hd)] * len(extras),
        out_specs=tile(hb, hd),
        out_shape=jax.ShapeDtypeStruct((B, H, nq, hd, t), BF16),
        scratch_shapes=[pltpu.VMEM((hb, t, t), F32), pltpu.VMEM((hb, t, t), BF16)],
        compiler_params=_params(("parallel", "parallel")),
        name="flash_window" if window is not None else "flash_causal",
    )(q, ka, v_t, masks, gate, *augs, *extras)


def _compress_kernel(x_ref, w1_ref, pe_ref, w1p_ref, w2k_ref, w2vt_ref, kc_ref, vct_ref):
    half = CMP_STRIDE * HEAD_DIM
    n_chunk = x_ref.shape[2] // CMP_STRIDE
    for kind in range(2):
        w1p = w1p_ref[kind]
        pe = pe_ref[kind].astype(BF16)
        bias = (_dot(pe[:, :half], w1p[:, :CMP_HIDDEN]) + _dot(pe[:, half:], w1p[:, CMP_HIDDEN:]))[0:1]
        ab = jnp.zeros((n_chunk, 2 * NSA_KV_HEADS * CMP_HIDDEN), F32)
        for p in range(CMP_STRIDE):
            x_p = x_ref[0, kind, pl.ds(p, n_chunk, stride=CMP_STRIDE), :].astype(BF16)
            ab = ab + _dot(x_p, w1_ref[kind, p])
        for g in range(NSA_KV_HEADS):
            lo = g * 2 * CMP_HIDDEN
            nxt = pltpu.roll(ab[:, lo + CMP_HIDDEN:lo + 2 * CMP_HIDDEN], n_chunk - 1, 0)
            pre = ab[:, lo:lo + CMP_HIDDEN] + nxt + bias
            hdn = (pre * jax.nn.sigmoid(pre)).astype(BF16)
            if kind == 0:
                kc_ref[0, g] = _dot(hdn, w2k_ref[...]).astype(BF16)
            else:
                vct_ref[0, g] = _dot_nt(w2vt_ref[...], hdn).astype(BF16)


def _compress(xc, w1exp, pe8, w1cat, w2k, w2vt):
    B, _, T, _ = xc.shape
    n_chunk = T // CMP_STRIDE
    full = lambda a: pl.BlockSpec(a.shape, lambda b: (0,) * a.ndim)
    return pl.pallas_call(
        _compress_kernel,
        grid=(B,),
        in_specs=[pl.BlockSpec((1,) + xc.shape[1:], lambda b: (b, 0, 0, 0)),
                  full(w1exp), full(pe8), full(w1cat), full(w2k), full(w2vt)],
        out_specs=[pl.BlockSpec((1, NSA_KV_HEADS, n_chunk, HEAD_DIM), lambda b: (b, 0, 0, 0)),
                   pl.BlockSpec((1, NSA_KV_HEADS, HEAD_DIM, n_chunk), lambda b: (b, 0, 0, 0))],
        out_shape=[jax.ShapeDtypeStruct((B, NSA_KV_HEADS, n_chunk, HEAD_DIM), BF16),
                   jax.ShapeDtypeStruct((B, NSA_KV_HEADS, HEAD_DIM, n_chunk), BF16)],
        compiler_params=_params(("parallel",)),
        name="nsa_compress",
    )(xc, w1exp, pe8, w1cat, w2k, w2vt)


def _cmp_attn_kernel(q_ref, kc_ref, vct_ref, g_ref, o_ref, sb_ref, *, tq, t):
    i = pl.program_id(2)
    n_cmp = kc_ref.shape[2]
    n_sel = sb_ref.shape[3]
    n_io = lax.broadcasted_iota(jnp.int32, (n_cmp, tq), 0)
    t_io = i * tq + lax.broadcasted_iota(jnp.int32, (n_cmp, tq), 1)
    valid = n_io * CMP_STRIDE + (CMP_LEN - 1) <= t_io
    kc = kc_ref[0, 0]
    vct = vct_ref[0, 0]
    psum = jnp.zeros((n_cmp, tq), F32)
    for r in range(NSA_REP):
        q = q_ref[0, r * HEAD_DIM:(r + 1) * HEAD_DIM, :]
        s = jnp.where(valid, _dot(kc, q), MASKED)
        m = jnp.max(s, axis=0, keepdims=True)
        e = jnp.where(valid, jnp.exp2(s - m), 0.0)
        l = jnp.sum(e, axis=0, keepdims=True)
        p = e * jnp.where(l > 0.0, 1.0 / l, 0.0)
        psum = psum + p
        o = _dot(vct, p.astype(BF16))
        o = (o * g_ref[0, 0, 0, r:r + 1, :]).astype(BF16)
        for c in range(tq // t):
            o_ref[0, r, c] = o[:, c * t:(c + 1) * t]

    jr = lax.broadcasted_iota(jnp.int32, (n_sel, n_cmp), 0)
    nc = lax.broadcasted_iota(jnp.int32, (n_sel, n_cmp), 1)
    overlap = ((nc * CMP_STRIDE < (jr + 1) * SEL_BLOCK)
               & (nc * CMP_STRIDE + CMP_LEN > jr * SEL_BLOCK)).astype(BF16)
    ph, pm, pl_ = _split3(psum)
    imp = _dot(overlap, ph) + _dot(overlap, pm) + _dot(overlap, pl_)
    j_io = lax.broadcasted_iota(jnp.int32, (n_sel, tq), 0)
    cur = jnp.right_shift(i * tq + lax.broadcasted_iota(jnp.int32, (n_sel, tq), 1),
                          SEL_BLOCK.bit_length() - 1)
    is_cur = j_io == cur
    is_fixed = (j_io == 0) | (j_io == cur - 1)
    imp = jnp.where(is_cur, 2.0 * FORCE_SCORE, jnp.where(is_fixed, FORCE_SCORE, imp))
    imp = jnp.where(j_io <= cur, imp, -1.0)
    grp = 8
    imp_g = [imp[k * grp:(k + 1) * grp] for k in range(n_sel // grp)]
    sub = lax.broadcasted_iota(jnp.int32, (grp, tq), 0)
    cnt_g = [jnp.zeros((grp, tq), F32) for _ in imp_g]
    for jp in range(n_sel):
        row = imp[jp:jp + 1, :]
        for k, x in enumerate(imp_g):
            ge = jnp.where(row >= x, 1.0, 0.0)
            gt = jnp.where(row > x, 1.0, 0.0)
            if k > jp // grp:
                beats = ge
            elif k < jp // grp:
                beats = gt
            else:
                beats = jnp.where(sub > jp % grp, ge, gt)
            cnt_g[k] = cnt_g[k] + beats
    cnt = jnp.concatenate(cnt_g, axis=0)
    sb = jnp.where(cnt < min(SEL_TOPK, n_sel), 0.0, MASKED).astype(BF16)
    for c in range(tq // t):
        sb_ref[0, 0, c] = sb[:, c * t:(c + 1) * t]


def _cmp_attn(nq_t, kc, vct, gate_cmp, n_sel, t, tq=1024):
    B, _, T = nq_t.shape
    G = kc.shape[1]
    rows = NSA_REP * HEAD_DIM
    return pl.pallas_call(
        functools.partial(_cmp_attn_kernel, tq=tq, t=t),
        grid=(B, G, T // tq),
        in_specs=[
            pl.BlockSpec((1, rows, tq), lambda b, g, i: (b, g, i)),
            pl.BlockSpec((1, 1) + kc.shape[2:], lambda b, g, i: (b, g, 0, 0)),
            pl.BlockSpec((1, 1) + vct.shape[2:], lambda b, g, i: (b, g, 0, 0)),
            pl.BlockSpec((1, 1, 1, NSA_REP, tq), lambda b, g, i: (b, 0, g, 0, i)),
        ],
        out_specs=[pl.BlockSpec((1, NSA_REP, tq // t, HEAD_DIM, t), lambda b, g, i: (b, g, i, 0, 0)),
                   pl.BlockSpec((1, 1, tq // t, n_sel, t), lambda b, g, i: (b, g, i, 0, 0))],
        out_shape=[jax.ShapeDtypeStruct((B, G * NSA_REP, T // t, HEAD_DIM, t), BF16),
                   jax.ShapeDtypeStruct((B, G, T // t, n_sel, t), BF16)],
        compiler_params=_params(("parallel", "parallel", "parallel")),
        name="nsa_cmp_attn",
    )(nq_t, kc, vct, gate_cmp)


def _mix_out_kernel(x_ref, of_ref, on_ref, wot_ref, g_ref, o_ref):
    heads, n_tiles, hd, t = of_ref.shape[1:]
    for c in range(n_tiles):
        of = of_ref[0, :, c].reshape(heads * hd, t)
        on = on_ref[0, :, c].reshape(heads * hd, t)
        mix_t = _dot(wot_ref[:, 0:FOX_QKV], of) + _dot(wot_ref[:, FOX_QKV:], on)
        rows = slice(c * t, (c + 1) * t)
        o_ref[0, rows, :] = x_ref[0, rows, :] + _rms(mix_t.T, g_ref[...])


def _mix_out(x, o_fox, o_nsa, wot, g, tm=512):
    B, T, D = x.shape
    heads, _, hd, t = o_fox.shape[1:]
    row = pl.BlockSpec((1, tm, D), lambda b, i: (b, i, 0))
    tiles = pl.BlockSpec((1, heads, tm // t, hd, t), lambda b, i: (b, 0, i, 0, 0))
    return pl.pallas_call(
        _mix_out_kernel,
        grid=(B, T // tm),
        in_specs=[row, tiles, tiles,
                  pl.BlockSpec(wot.shape, lambda b, i: (0, 0)),
                  pl.BlockSpec((1, D), lambda b, i: (0, 0))],
        out_specs=row,
        out_shape=jax.ShapeDtypeStruct((B, T, D), F32),
        compiler_params=_params(("parallel", "parallel")),
        name="mix_out",
    )(x, o_fox, o_nsa, wot, g)


def _mem_kv_kernel(m_ref, g_ref, w_ref, k_ref, v_ref):
    m = _rms(m_ref[0], g_ref[...]).astype(BF16)
    k_ref[0] = _dot(m, w_ref[:, :D_MODEL]).astype(BF16)
    v_ref[0] = _dot(m, w_ref[:, D_MODEL:]).astype(BF16)


def _mem_kv(mem, g, wkv):
    B, M, D = mem.shape
    blk = pl.BlockSpec((1, M, D), lambda b: (b, 0, 0))
    return pl.pallas_call(
        _mem_kv_kernel,
        grid=(B,),
        in_specs=[blk, pl.BlockSpec((1, D), lambda b: (0, 0)), pl.BlockSpec(wkv.shape, lambda b: (0, 0))],
        out_specs=[blk, blk],
        out_shape=[jax.ShapeDtypeStruct((B, M, D), BF16)] * 2,
        compiler_params=_params(("parallel",)),
        name="mem_kv",
    )(mem, g, wkv)


def _cross_kernel(h_ref, k_ref, v_ref, wq_ref, wo_ref, gpre_ref, gpost_ref, o_ref):
    h = h_ref[0]
    n = _rms(h, gpre_ref[...]).astype(BF16)
    scale = CROSS_HEAD_DIM ** -0.5
    q_all = (_dot(n, wq_ref[...]) * scale).astype(BF16)
    heads = []
    for hh in range(CROSS_HEADS):
        sl = slice(hh * CROSS_HEAD_DIM, (hh + 1) * CROSS_HEAD_DIM)
        s = _dot_nt(q_all[:, sl], k_ref[0, :, sl])
        m = jnp.max(s, axis=-1, keepdims=True)
        p = jnp.exp(s - m)
        l = jnp.sum(p, axis=-1, keepdims=True)
        heads.append((_dot(p.astype(BF16), v_ref[0, :, sl]) / l).astype(BF16))
    out = _dot(jnp.concatenate(heads, axis=-1), wo_ref[...])
    o_ref[0] = h + _rms(out, gpost_ref[...])


def _cross(h, k, v, wq, wo, g_pre, g_post, tm=512):
    B, T, D = h.shape
    M = k.shape[1]
    row = pl.BlockSpec((1, tm, D), lambda b, i: (b, i, 0))
    kvb = pl.BlockSpec((1, M, D), lambda b, i: (b, 0, 0))
    wsp = pl.BlockSpec((D, D), lambda b, i: (0, 0))
    gsp = pl.BlockSpec((1, D), lambda b, i: (0, 0))
    return pl.pallas_call(
        _cross_kernel,
        grid=(B, T // tm),
        in_specs=[row, kvb, kvb, wsp, wsp, gsp, gsp],
        out_specs=row,
        out_shape=jax.ShapeDtypeStruct((B, T, D), F32),
        compiler_params=_params(("parallel", "parallel")),
        name="mem_cross",
    )(h, k, v, wq, wo, g_pre, g_post)


def _mlp_kernel(h_ref, wu_ref, wd_ref, gpre_ref, gpost_ref, o_ref, *, hc):
    h = h_ref[...]
    n = _rms(h, gpre_ref[...]).astype(BF16)
    acc = jnp.zeros(h.shape, F32)
    for c in range(wu_ref.shape[1] // hc):
        u = jnp.maximum(_dot(n, wu_ref[:, c * hc:(c + 1) * hc]), 0.0)
        acc = acc + _dot((u * u).astype(BF16), wd_ref[c * hc:(c + 1) * hc, :])
    o_ref[...] = h + _rms(acc, gpost_ref[...])


def _mlp(h, wu, wd, g_pre, g_post, tm=512, hc=512):
    N, D = h.shape
    row = pl.BlockSpec((tm, D), lambda i: (i, 0))
    gsp = pl.BlockSpec((1, D), lambda i: (0, 0))
    once = pl.Buffered(1)
    return pl.pallas_call(
        functools.partial(_mlp_kernel, hc=hc),
        grid=(N // tm,),
        in_specs=[row,
                  pl.BlockSpec(wu.shape, lambda i: (0, 0), pipeline_mode=once),
                  pl.BlockSpec(wd.shape, lambda i: (0, 0), pipeline_mode=once),
                  gsp, gsp],
        out_specs=row,
        out_shape=jax.ShapeDtypeStruct((N, D), F32),
        compiler_params=_params(("parallel",)),
        name="relu2_mlp",
    )(h, wu, wd, g_pre, g_post)


def _layer(h, mem, g_mix_pre, w_in, b_forget, w_ck1, w_ck2, w_cv1, w_cv2, pe_k, pe_v,
           w_mix_out, g_mix_post, g_x_pre, g_mem, w_xq, w_xkv, w_xo, g_x_post,
           g_mlp_pre, w_up, w_down, g_mlp_post):
    B, T, D = h.shape
    H, G, hd = NSA_HEADS, NSA_KV_HEADS, HEAD_DIM
    row = lambda g: g.reshape(1, -1)

    cols = {}
    lo = 0
    for name, size in (("fq", FOX_QKV), ("fk", FOX_QKV), ("fv", FOX_QKV), ("ff", FOX_HEADS), ("nq", NSA_Q),
                       ("kc", NSA_KV), ("vc", NSA_KV), ("ks", NSA_KV), ("vs", NSA_KV), ("kw", NSA_KV),
                       ("vw", NSA_KV), ("ng", NSA_HEADS * N_BRANCH)):
        cols[name] = w_in[:, lo:lo + size]
        lo += size
    ng_branch_major = cols["ng"].reshape(D, NSA_HEADS, N_BRANCH).transpose(0, 2, 1).reshape(D, -1)
    wt = jnp.concatenate([cols[k] for k in ("fq", "fv", "nq", "vs", "vw", "ff")]
                         + [ng_branch_major], axis=1).T.astype(BF16)
    wk = jnp.concatenate([cols[k] for k in ("fk", "kc", "vc", "ks", "kw")], axis=1).astype(BF16)
    half = hd // 2
    inv = ROPE_THETA ** (-jnp.arange(half, dtype=F32) / half)
    ang = inv[:, None] * jnp.arange(T, dtype=F32)[None, :]
    cos_t, sin_t = jnp.cos(ang), jnp.sin(ang)
    cos_k, sin_k = jnp.tile(cos_t.T, (1, 2 * G)), jnp.tile(sin_t.T, (1, 2 * G))

    def slot_placement(heads):
        src = jnp.arange(heads * hd)
        return (jnp.arange(heads * AUG)[None, :] == ((src // hd) * AUG + src % hd)[:, None]).astype(F32)

    d = jnp.arange(G * hd)
    partner = jnp.where(d % hd < half, d + half, d - half)
    rot = (jnp.arange(G * hd)[:, None] == partner[None, :]) * jnp.where(d % hd < half, -1.0, 1.0)[None, :]
    place_rope = jnp.concatenate([slot_placement(G), rot @ slot_placement(G)], axis=0).astype(BF16)
    place_fox = slot_placement(FOX_HEADS).astype(BF16)

    t = 256
    hb = 8
    nq = T // t
    (fq, k_fox, fv, nq_t, nqr, kcvc, ka_slc, vs, ka_win, vw, lf_t, gt_t, gate_tiles) = _in_proj(
        h, row(g_mix_pre), wt, wk, b_forget.reshape(FOX_HEADS, 1), cos_t, sin_t, cos_k, sin_k,
        place_fox, place_rope, t)
    gate_h = gate_tiles.reshape(B, N_BRANCH, H, nq, 1, t)

    aug_fox, ka_fox = _decay(lf_t, k_fox, t)
    one_gate = jnp.ones((B, FOX_HEADS, nq, 1, t), F32)
    o_fox = _flash(fq, ka_fox, fv, one_gate, aug_fox, hb=hb)

    hsz = CMP_STRIDE * hd
    w1cat = jnp.stack([jnp.concatenate([w[:hsz], w[hsz:]], axis=1) for w in (w_ck1, w_cv1)]).astype(BF16)
    w1exp = jnp.zeros((2, CMP_STRIDE, G, hd, G, 2, CMP_HIDDEN), F32)
    for kind, w in enumerate((w_ck1, w_cv1)):
        halves = w.reshape(2, CMP_STRIDE, hd, CMP_HIDDEN).transpose(1, 2, 0, 3)
        for g in range(G):
            w1exp = w1exp.at[kind, :, g, :, g].set(halves)
    w1exp = w1exp.reshape(2, CMP_STRIDE, G * hd, G * 2 * CMP_HIDDEN).astype(BF16)
    pe8 = jnp.stack([jnp.broadcast_to(p.reshape(1, -1), (8, CMP_LEN * hd)) for p in (pe_k, pe_v)])
    kc, vct = _compress(kcvc, w1exp, pe8, w1cat, w_ck2.astype(BF16), w_cv2.T.astype(BF16))

    gates = gt_t.reshape(B, N_BRANCH, G, NSA_REP, T)
    n_sel = T // SEL_BLOCK
    o_cmp, selbias = _cmp_attn(nq_t, kc, vct, gates[:, 0:1], n_sel, t)

    o_slc = _flash(nqr, ka_slc, vs, gate_h[:, 1], selbias, hb=hb)
    o_nsa = _flash(nqr, ka_win, vw, gate_h[:, 2], extras=(o_cmp, o_slc), hb=hb, window=WINDOW)

    h = _mix_out(h, o_fox, o_nsa, w_mix_out.T.astype(BF16), row(g_mix_post))

    k_mem, v_mem = _mem_kv(mem, row(g_mem), w_xkv.astype(BF16))
    h = _cross(h, k_mem, v_mem, w_xq.astype(BF16), w_xo.astype(BF16), row(g_x_pre), row(g_x_post))

    h = _mlp(h.reshape(B * T, D), w_up.astype(BF16), w_down.astype(BF16),
             row(g_mlp_pre), row(g_mlp_post)).reshape(B, T, D)
    return h


def kernel(x, mem, g_mix_pre, w_in, b_forget, w_ck1, w_ck2, w_cv1, w_cv2, pe_k, pe_v, w_mix_out, g_mix_post,
           g_x_pre, g_mem, w_xq, w_xkv, w_xo, g_x_post, g_mlp_pre, w_up, w_down, g_mlp_post):
    h = x
    for l in range(g_mix_pre.shape[0]):
        h = _layer(h, mem, g_mix_pre[l], w_in[l], b_forget[l], w_ck1[l], w_ck2[l], w_cv1[l], w_cv2[l],
                   pe_k[l], pe_v[l], w_mix_out[l], g_mix_post[l], g_x_pre[l], g_mem[l], w_xq[l], w_xkv[l],
                   w_xo[l], g_x_post[l], g_mlp_pre[l], w_up[l], w_down[l], g_mlp_post[l])
    return h
```

```python
import functools
import math

import jax
import jax.numpy as jnp
from jax import lax
from jax.experimental import pallas as pl
from jax.experimental.pallas import tpu as pltpu

D_MODEL = 1024
HEAD_DIM = 64
FOX_HEADS = 8
NSA_HEADS = 8
NSA_KV_HEADS = 2
NSA_REP = NSA_HEADS // NSA_KV_HEADS
CMP_LEN = 32
CMP_STRIDE = 16
CMP_HIDDEN = 2 * HEAD_DIM
SEL_BLOCK = 64
SEL_TOPK = 16
WINDOW = 512
N_BRANCH = 3
CROSS_HEADS = 4
CROSS_HEAD_DIM = D_MODEL // CROSS_HEADS
MLP_HIDDEN = 4 * D_MODEL
ROPE_THETA = 10000.0
RMS_EPS = 1e-6
FORCE_SCORE = 1e4
MASKED = -1e30
LOG2E = math.log2(math.e)

FOX_QKV = FOX_HEADS * HEAD_DIM
NSA_Q = NSA_HEADS * HEAD_DIM
NSA_KV = NSA_KV_HEADS * HEAD_DIM
AUG = 128

V7X_VMEM_LIMIT = 56 * 1024 * 1024

F32 = jnp.float32
BF16 = jnp.bfloat16


def _params(sem, vmem=V7X_VMEM_LIMIT):
    return pltpu.CompilerParams(dimension_semantics=sem, vmem_limit_bytes=vmem)


def _rms(x, g):
    return x * lax.rsqrt(jnp.mean(x * x, axis=-1, keepdims=True) + RMS_EPS) * g


def _dot(a, b):
    return jnp.dot(a, b, preferred_element_type=F32)


def _dot_nt(a, b):
    return lax.dot_general(a, b, (((1,), (1,)), ((), ())), preferred_element_type=F32)


def _split3(x):
    hi = x.astype(BF16)
    r1 = x - hi.astype(F32)
    mid = r1.astype(BF16)
    lo = (r1 - mid.astype(F32)).astype(BF16)
    return hi, mid, lo


_R_FQ, _R_FV, _R_NQ, _R_VS, _R_VW, _R_FF, _R_NG, _R_END = 0, 512, 1024, 1536, 1664, 1792, 1800, 1824
_C_FK, _C_KC, _C_KS, _C_KW, _C_END = 0, 512, 768, 896, 1024
V_ROWS = HEAD_DIM + 16


def _in_proj_kernel(x_ref, g_ref, wt_ref, wk_ref, bf_ref, cos_ref, sin_ref, cosk_ref, sink_ref, pfox_ref, prope_ref,
                    fq_ref, kfox_ref, fv_ref, nq_ref, nqr_ref, kcvc_ref,
                    kslc_ref, vs_ref, kwin_ref, vw_ref, lf_ref, gt_ref, gtt_ref, *, t):
    n = _rms(x_ref[0], g_ref[...]).astype(BF16)
    tm = n.shape[0]
    cos = cos_ref[...]
    sin = sin_ref[...]
    half = HEAD_DIM // 2

    def proj(lo, hi):
        return _dot_nt(wt_ref[lo:hi, :], n)

    def rope(r):
        parts = []
        for h in range(r.shape[0] // HEAD_DIM):
            x1 = r[h * HEAD_DIM:h * HEAD_DIM + half]
            x2 = r[h * HEAD_DIM + half:(h + 1) * HEAD_DIM]
            parts += [x1 * cos - x2 * sin, x2 * cos + x1 * sin]
        return jnp.concatenate(parts, axis=0)

    def store_tiles(r, out_ref):
        r = r.astype(BF16)
        rows = out_ref.shape[3]
        if rows > HEAD_DIM:
            sub = lax.broadcasted_iota(jnp.int32, (rows - HEAD_DIM, t), 0)
            tail = jnp.where(sub == 0, 1.0, 0.0).astype(BF16)
        for h in range(r.shape[0] // HEAD_DIM):
            for c in range(tm // t):
                tile = r[h * HEAD_DIM:(h + 1) * HEAD_DIM, c * t:(c + 1) * t]
                out_ref[0, h, c] = tile if rows == HEAD_DIM else jnp.concatenate([tile, tail], axis=0)

    scale = HEAD_DIM ** -0.5 * LOG2E
    store_tiles(proj(_R_FQ, _R_FV) * scale, fq_ref)
    store_tiles(proj(_R_FV, _R_NQ), fv_ref)
    nq = proj(_R_NQ, _R_VS) * scale
    nq_ref[0] = nq.astype(BF16)
    store_tiles(rope(nq), nqr_ref)
    store_tiles(proj(_R_VS, _R_VW), vs_ref)
    store_tiles(proj(_R_VW, _R_FF), vw_ref)
    small = proj(_R_FF, _R_END)
    z = small[0:FOX_HEADS] + bf_ref[...]
    lf_ref[0] = jnp.minimum(z, 0.0) - jnp.log1p(jnp.exp(-jnp.abs(z)))
    gates = jax.nn.sigmoid(small[FOX_HEADS:])
    gt_ref[0] = gates
    for r in range(gates.shape[0]):
        for c in range(tm // t):
            gtt_ref[0, r, c] = gates[r:r + 1, c * t:(c + 1) * t]

    tok = _dot(n, wk_ref[...])
    kfox_ref[0] = _dot(tok[:, _C_FK:_C_KC].astype(BF16), pfox_ref[...]).astype(BF16)
    kcvc_ref[0, 0] = tok[:, _C_KC:_C_KC + NSA_KV]
    kcvc_ref[0, 1] = tok[:, _C_KC + NSA_KV:_C_KS]

    def rope_placed(k):
        both = jnp.concatenate([(k * cosk_ref[...]).astype(BF16), (k * sink_ref[...]).astype(BF16)], axis=1)
        return _dot(both, prope_ref[...])

    lane = lax.broadcasted_iota(jnp.int32, (tm, NSA_KV_HEADS * AUG), 1) & (AUG - 1)
    pos = pl.program_id(1) * tm + lax.broadcasted_iota(jnp.int32, (tm, NSA_KV_HEADS * AUG), 0)
    block_col = lane == HEAD_DIM + jnp.right_shift(pos, SEL_BLOCK.bit_length() - 1)
    kslc_ref[0] = (rope_placed(tok[:, _C_KS:_C_KW]) + jnp.where(block_col, 1.0, 0.0)).astype(BF16)
    kwin_ref[0] = rope_placed(tok[:, _C_KW:_C_END]).astype(BF16)


def _in_proj(x, g, wt, wk, b_forget, cos_t, sin_t, cos_k, sin_k, place_fox, place_rope, t, tm=512):
    B, T, D = x.shape
    grid = (B, T // tm)
    chan = lambda c, dt: (jax.ShapeDtypeStruct((B, c, T), dt), pl.BlockSpec((1, c, tm), lambda b, i: (b, 0, i)))
    toks = lambda c: (jax.ShapeDtypeStruct((B, T, c), BF16), pl.BlockSpec((1, tm, c), lambda b, i: (b, i, 0)))
    tiles = lambda h, rows, dt=BF16: (jax.ShapeDtypeStruct((B, h, T // t, rows, t), dt),
                                      pl.BlockSpec((1, h, tm // t, rows, t), lambda b, i: (b, 0, i, 0, 0)))
    n_gate = NSA_HEADS * N_BRANCH
    outs = [tiles(FOX_HEADS, HEAD_DIM), toks(FOX_HEADS * AUG), tiles(FOX_HEADS, V_ROWS), chan(NSA_Q, BF16),
            tiles(NSA_HEADS, HEAD_DIM),
            (jax.ShapeDtypeStruct((B, 2, T, NSA_KV), F32), pl.BlockSpec((1, 2, tm, NSA_KV), lambda b, i: (b, 0, i, 0))),
            toks(NSA_KV_HEADS * AUG), tiles(NSA_KV_HEADS, V_ROWS), toks(NSA_KV_HEADS * AUG),
            tiles(NSA_KV_HEADS, V_ROWS), chan(FOX_HEADS, F32), chan(n_gate, F32), tiles(n_gate, 1, F32)]
    full = lambda a: pl.BlockSpec(a.shape, lambda b, i: (0,) * a.ndim)
    return pl.pallas_call(
        functools.partial(_in_proj_kernel, t=t),
        grid=grid,
        in_specs=[
            pl.BlockSpec((1, tm, D), lambda b, i: (b, i, 0)),
            full(g), full(wt), full(wk), full(b_forget),
            pl.BlockSpec((HEAD_DIM // 2, tm), lambda b, i: (0, i)),
            pl.BlockSpec((HEAD_DIM // 2, tm), lambda b, i: (0, i)),
            pl.BlockSpec((tm, NSA_KV), lambda b, i: (i, 0)),
            pl.BlockSpec((tm, NSA_KV), lambda b, i: (i, 0)),
            full(place_fox), full(place_rope),
        ],
        out_specs=[o[1] for o in outs],
        out_shape=[o[0] for o in outs],
        compiler_params=_params(("parallel", "parallel")),
        name="in_proj",
    )(x, g, wt, wk, b_forget, cos_t, sin_t, cos_k, sin_k, place_fox, place_rope)


_CS = 128
Q_AUG_FOX = 16


def _decay_kernel(x_ref, k_ref, qa_ref, ka_ref, *, t):
    x = x_ref[0]
    H, T = x.shape
    r = lax.broadcasted_iota(jnp.int32, (_CS, _CS), 0)
    c = lax.broadcasted_iota(jnp.int32, (_CS, _CS), 1)
    tri = (r <= c).astype(BF16)
    carry = jnp.zeros((H, 1), F32)
    chunks = []
    for ch in range(T // _CS):
        xh, xm, xl = _split3(x[:, ch * _CS:(ch + 1) * _CS])
        y = _dot(xh, tri) + _dot(xm, tri) + _dot(xl, tri) + carry
        carry = y[:, _CS - 1:_CS]
        chunks.append(y)
    cs = jnp.concatenate(chunks, axis=1) * LOG2E
    hi, mid, lo = _split3(cs)
    parts = jnp.concatenate([hi.astype(F32), mid.astype(F32), lo.astype(F32), jnp.zeros((H, T), F32)], axis=0)

    n_q = H * Q_AUG_FOX
    row = lax.broadcasted_iota(jnp.int32, (n_q, 4 * H), 0)
    src = lax.broadcasted_iota(jnp.int32, (n_q, 4 * H), 1)
    sh = Q_AUG_FOX.bit_length() - 1
    place_q = ((src == (row & (Q_AUG_FOX - 1)) * H + jnp.right_shift(row, sh)) & ((row & (Q_AUG_FOX - 1)) < 3))
    rq = lax.broadcasted_iota(jnp.int32, (n_q, T), 0) & (Q_AUG_FOX - 1)
    qa = _dot(place_q.astype(BF16), parts.astype(BF16)) + jnp.where((rq >= 3) & (rq < 6), 1.0, 0.0)
    qa = qa.astype(BF16)
    for h in range(H):
        for i in range(T // t):
            qa_ref[0, h, i] = qa[h * Q_AUG_FOX:(h + 1) * Q_AUG_FOX, i * t:(i + 1) * t]

    n_k = H * AUG
    src = lax.broadcasted_iota(jnp.int32, (4 * H, n_k), 0)
    col = lax.broadcasted_iota(jnp.int32, (4 * H, n_k), 1)
    ck = (col & (AUG - 1)) - HEAD_DIM
    hk = jnp.right_shift(col, AUG.bit_length() - 1)
    place_k = jnp.where((src == (ck - 3) * H + hk) & (ck >= 3) & (ck < 6), -1.0, 0.0).astype(BF16)
    ckt = (lax.broadcasted_iota(jnp.int32, (t, n_k), 1) & (AUG - 1)) - HEAD_DIM
    ones_k = jnp.where((ckt >= 0) & (ckt < 3), 1.0, 0.0)
    parts_t = parts.T.astype(BF16)
    for i in range(T // t):
        rows = slice(i * t, (i + 1) * t)
        ka_ref[0, rows, :] = (k_ref[0, rows, :].astype(F32) + _dot(parts_t[rows], place_k) + ones_k).astype(BF16)


def _decay(lf, k_placed, t):
    B, H, T = lf.shape
    kspec = pl.BlockSpec((1, T, H * AUG), lambda b: (b, 0, 0))
    return pl.pallas_call(
        functools.partial(_decay_kernel, t=t),
        grid=(B,),
        in_specs=[pl.BlockSpec((1, H, T), lambda b: (b, 0, 0)), kspec],
        out_specs=[pl.BlockSpec((1, H, T // t, Q_AUG_FOX, t), lambda b: (b, 0, 0, 0, 0)), kspec],
        out_shape=[jax.ShapeDtypeStruct((B, H, T // t, Q_AUG_FOX, t), BF16),
                   jax.ShapeDtypeStruct(k_placed.shape, BF16)],
        input_output_aliases={1: 1},
        compiler_params=_params(("parallel",)),
        name="decay_cumsum",
    )(lf, k_placed)


def _flash_kernel(q_ref, k_ref, v_ref, mask_ref, g_ref, *rest, t, back, n_aug, n_extra):
    aug_ref = rest[0] if n_aug else None
    extras = rest[n_aug:n_aug + n_extra]
    o_ref, s_ref, p_ref, acc_ref = rest[n_aug + n_extra:]
    hb, nq, hd = q_ref.shape[1], q_ref.shape[2], q_ref.shape[3]
    rep = hb // v_ref.shape[1]
    n_pairs = sum(min(i, back) + 1 for i in range(nq))

    def masked_scores(h, i, j):
        rows = [q_ref[0, h, i]]
        if aug_ref is not None:
            rows.append(aug_ref[0, h if aug_ref.shape[1] == hb else h // rep, i])
        pad = AUG - sum(r.shape[0] for r in rows)
        q = jnp.concatenate(rows + [jnp.zeros((pad, t), BF16)], axis=0)
        kh = h // rep
        k = k_ref[0, pl.ds(pl.multiple_of(j * t, t), t), kh * AUG:(kh + 1) * AUG]
        s = _dot(k, q) + mask_ref[jnp.maximum(j - i + (mask_ref.shape[0] - 1), 0)]
        s_ref[h] = s
        return jnp.max(s, axis=0, keepdims=True)

    def weighted_values(h, j):
        return _dot(v_ref[0, h // rep, j], p_ref[h])

    def emit(h, i, acc):
        l = acc[hd:hd + 1]
        o_ref[0, h, i] = (acc[:hd] * jnp.where(l > 0.0, g_ref[0, h, i] / l, 0.0)).astype(o_ref.dtype)

    tile_max = []
    for h in range(hb):
        tile_max.append(masked_scores(h, 0, 0))
        p_ref[h] = jnp.zeros((t, t), BF16)
        acc_ref[h] = jnp.zeros(acc_ref.shape[1:], F32)

    def body(_, carry):
        (i, j, i_prev, j_prev), heads = carry
        row_end = j == i
        i_next = jnp.minimum(jnp.where(row_end, i + 1, i), nq - 1)
        j_next = jnp.where(row_end, jnp.maximum(i_next - back, 0), j + 1)
        row_start = j == jnp.maximum(i - back, 0)
        out = []
        for h in range(hb):
            m, alpha, s_max = heads[h]
            acc = acc_ref[h] * alpha + weighted_values(h, j_prev)
            acc_ref[h] = acc
            emit(h, i_prev, acc)
            m = jnp.where(row_start, MASKED, m)
            m_new = jnp.maximum(m, s_max)
            alpha = jnp.exp2(m - m_new)
            p_ref[h] = jnp.exp2(s_ref[h] - m_new).astype(BF16)
            s_max = masked_scores(h, i_next, j_next)
            out.append((m_new, alpha, s_max))
        return (i_next, j_next, i, j), tuple(out)

    zero = jnp.int32(0)
    heads = tuple((jnp.full((1, t), MASKED, F32), jnp.ones((1, t), F32), tile_max[h]) for h in range(hb))
    _, heads = lax.fori_loop(0, n_pairs, body, ((zero, zero, zero, zero), heads))
    for h in range(hb):
        m, alpha, _ = heads[h]
        emit(h, nq - 1, acc_ref[h] * alpha + weighted_values(h, nq - 1))
    for x_ref in extras:
        for h in range(hb):
            for i in range(nq):
                o_ref[0, h, i] = (o_ref[0, h, i].astype(F32) + x_ref[0, h, i].astype(F32)).astype(o_ref.dtype)


def _mask_table(t, n, window):
    k = jnp.arange(t)[:, None]
    q = jnp.arange(t)[None, :]
    tiles = []
    for d in range(n):
        off = (d - (n - 1)) * t
        ok = (k + off) <= q
        if window is not None:
            ok = ok & ((k + off) > q - window)
        tiles.append(ok)
    return jnp.where(jnp.stack(tiles), 0.0, MASKED).astype(F32)


def _flash(q, ka, v_t, gate, aug=None, extras=(), *, hb, window=None):
    B, H, nq, hd, t = q.shape
    G = v_t.shape[1]
    T = nq * t
    assert hb % (H // G) == 0
    kb = hb // (H // G)
    if window is None:
        back = nq - 1
        masks = _mask_table(t, 2, None)
    else:
        assert window % t == 0
        back = window // t
        masks = _mask_table(t, back + 1, window)
    tile = lambda n, c: pl.BlockSpec((1, n, nq, c, t), lambda b, h: (b, h, 0, 0, 0))
    augs = () if aug is None else (aug,)
    aug_specs = [tile(hb if a.shape[1] == H else kb, a.shape[3]) for a in augs]
    return pl.pallas_call(
        functools.partial(_flash_kernel, t=t, back=back, n_aug=len(augs), n_extra=len(extras)),
        grid=(B, H // hb),
        in_specs=[
            tile(hb, hd),
            pl.BlockSpec((1, T, kb * AUG), lambda b, h: (b, 0, h)),
            tile(kb, v_t.shape[3]),
            pl.BlockSpec(masks.shape, lambda b, h: (0, 0, 0)),
            tile(hb, 1),
        ] + aug_specs + [tile(hb, hd)] * len(extras),
        out_specs=tile(hb, hd),
        out_shape=jax.ShapeDtypeStruct((B, H, nq, hd, t), BF16),
        scratch_shapes=[pltpu.VMEM((hb, t, t), F32), pltpu.VMEM((hb, t, t), BF16),
                        pltpu.VMEM((hb, v_t.shape[3], t), F32)],
        compiler_params=_params(("parallel", "parallel")),
        name="flash_window" if window is not None else "flash_causal",
    )(q, ka, v_t, masks, gate, *augs, *extras)


def _compress_kernel(x_ref, w1_ref, pe_ref, w1p_ref, w2k_ref, w2vt_ref, kc_ref, vct_ref):
    half = CMP_STRIDE * HEAD_DIM
    n_chunk = x_ref.shape[2] // CMP_STRIDE
    for kind in range(2):
        w1p = w1p_ref[kind]
        pe = pe_ref[kind].astype(BF16)
        bias = (_dot(pe[:, :half], w1p[:, :CMP_HIDDEN]) + _dot(pe[:, half:], w1p[:, CMP_HIDDEN:]))[0:1]
        ab = jnp.zeros((n_chunk, 2 * NSA_KV_HEADS * CMP_HIDDEN), F32)
        for p in range(CMP_STRIDE):
            x_p = x_ref[0, kind, pl.ds(p, n_chunk, stride=CMP_STRIDE), :].astype(BF16)
            ab = ab + _dot(x_p, w1_ref[kind, p])
        for g in range(NSA_KV_HEADS):
            lo = g * 2 * CMP_HIDDEN
            nxt = pltpu.roll(ab[:, lo + CMP_HIDDEN:lo + 2 * CMP_HIDDEN], n_chunk - 1, 0)
            pre = ab[:, lo:lo + CMP_HIDDEN] + nxt + bias
            hdn = (pre * jax.nn.sigmoid(pre)).astype(BF16)
            if kind == 0:
                kc_ref[0, g] = _dot(hdn, w2k_ref[...]).astype(BF16)
            else:
                vct_ref[0, g] = _dot_nt(w2vt_ref[...], hdn).astype(BF16)


def _compress(xc, w1exp, pe8, w1cat, w2k, w2vt):
    B, _, T, _ = xc.shape
    n_chunk = T // CMP_STRIDE
    full = lambda a: pl.BlockSpec(a.shape, lambda b: (0,) * a.ndim)
    return pl.pallas_call(
        _compress_kernel,
        grid=(B,),
        in_specs=[pl.BlockSpec((1,) + xc.shape[1:], lambda b: (b, 0, 0, 0)),
                  full(w1exp), full(pe8), full(w1cat), full(w2k), full(w2vt)],
        out_specs=[pl.BlockSpec((1, NSA_KV_HEADS, n_chunk, HEAD_DIM), lambda b: (b, 0, 0, 0)),
                   pl.BlockSpec((1, NSA_KV_HEADS, HEAD_DIM, n_chunk), lambda b: (b, 0, 0, 0))],
        out_shape=[jax.ShapeDtypeStruct((B, NSA_KV_HEADS, n_chunk, HEAD_DIM), BF16),
                   jax.ShapeDtypeStruct((B, NSA_KV_HEADS, HEAD_DIM, n_chunk), BF16)],
        compiler_params=_params(("parallel",)),
        name="nsa_compress",
    )(xc, w1exp, pe8, w1cat, w2k, w2vt)


def _cmp_attn_kernel(q_ref, kc_ref, vct_ref, g_ref, o_ref, sb_ref, *, tq, t):
    i = pl.program_id(2)
    n_cmp = kc_ref.shape[2]
    n_sel = sb_ref.shape[3]
    n_io = lax.broadcasted_iota(jnp.int32, (n_cmp, tq), 0)
    t_io = i * tq + lax.broadcasted_iota(jnp.int32, (n_cmp, tq), 1)
    valid = n_io * CMP_STRIDE + (CMP_LEN - 1) <= t_io
    kc = kc_ref[0, 0]
    vct = vct_ref[0, 0]
    psum = jnp.zeros((n_cmp, tq), F32)
    for r in range(NSA_REP):
        q = q_ref[0, r * HEAD_DIM:(r + 1) * HEAD_DIM, :]
        s = jnp.where(valid, _dot(kc, q), MASKED)
        m = jnp.max(s, axis=0, keepdims=True)
        e = jnp.where(valid, jnp.exp2(s - m), 0.0)
        l = jnp.sum(e, axis=0, keepdims=True)
        p = e * jnp.where(l > 0.0, 1.0 / l, 0.0)
        psum = psum + p
        o = _dot(vct, p.astype(BF16))
        o = (o * g_ref[0, 0, 0, r:r + 1, :]).astype(BF16)
        for c in range(tq // t):
            o_ref[0, r, c] = o[:, c * t:(c + 1) * t]

    jr = lax.broadcasted_iota(jnp.int32, (n_sel, n_cmp), 0)
    nc = lax.broadcasted_iota(jnp.int32, (n_sel, n_cmp), 1)
    overlap = ((nc * CMP_STRIDE < (jr + 1) * SEL_BLOCK)
               & (nc * CMP_STRIDE + CMP_LEN > jr * SEL_BLOCK)).astype(BF16)
    ph, pm, pl_ = _split3(psum)
    imp = _dot(overlap, ph) + _dot(overlap, pm) + _dot(overlap, pl_)
    j_io = lax.broadcasted_iota(jnp.int32, (n_sel, tq), 0)
    cur = jnp.right_shift(i * tq + lax.broadcasted_iota(jnp.int32, (n_sel, tq), 1),
                          SEL_BLOCK.bit_length() - 1)
    is_cur = j_io == cur
    is_fixed = (j_io == 0) | (j_io == cur - 1)
    imp = jnp.where(is_cur, 2.0 * FORCE_SCORE, jnp.where(is_fixed, FORCE_SCORE, imp))
    imp = jnp.where(j_io <= cur, imp, -1.0)
    grp = 8
    imp_g = [imp[k * grp:(k + 1) * grp] for k in range(n_sel // grp)]
    sub = lax.broadcasted_iota(jnp.int32, (grp, tq), 0)
    cnt_g = [jnp.zeros((grp, tq), F32) for _ in imp_g]
    for jp in range(n_sel):
        row = imp[jp:jp + 1, :]
        for k, x in enumerate(imp_g):
            ge = jnp.where(row >= x, 1.0, 0.0)
            gt = jnp.where(row > x, 1.0, 0.0)
            if k > jp // grp:
                beats = ge
            elif k < jp // grp:
                beats = gt
            else:
                beats = jnp.where(sub > jp % grp, ge, gt)
            cnt_g[k] = cnt_g[k] + beats
    cnt = jnp.concatenate(cnt_g, axis=0)
    sb = jnp.where(cnt < min(SEL_TOPK, n_sel), 0.0, MASKED).astype(BF16)
    for c in range(tq // t):
        sb_ref[0, 0, c] = sb[:, c * t:(c + 1) * t]


def _cmp_attn(nq_t, kc, vct, gate_cmp, n_sel, t, tq=1024):
    B, _, T = nq_t.shape
    G = kc.shape[1]
    rows = NSA_REP * HEAD_DIM
    return pl.pallas_call(
        functools.partial(_cmp_attn_kernel, tq=tq, t=t),
        grid=(B, G, T // tq),
        in_specs=[
            pl.BlockSpec((1, rows, tq), lambda b, g, i: (b, g, i)),
            pl.BlockSpec((1, 1) + kc.shape[2:], lambda b, g, i: (b, g, 0, 0)),
            pl.BlockSpec((1, 1) + vct.shape[2:], lambda b, g, i: (b, g, 0, 0)),
            pl.BlockSpec((1, 1, 1, NSA_REP, tq), lambda b, g, i: (b, 0, g, 0, i)),
        ],
        out_specs=[pl.BlockSpec((1, NSA_REP, tq // t, HEAD_DIM, t), lambda b, g, i: (b, g, i, 0, 0)),
                   pl.BlockSpec((1, 1, tq // t, n_sel, t), lambda b, g, i: (b, g, i, 0, 0))],
        out_shape=[jax.ShapeDtypeStruct((B, G * NSA_REP, T // t, HEAD_DIM, t), BF16),
                   jax.ShapeDtypeStruct((B, G, T // t, n_sel, t), BF16)],
        compiler_params=_params(("parallel", "parallel", "parallel")),
        name="nsa_cmp_attn",
    )(nq_t, kc, vct, gate_cmp)


def _mix_out_kernel(x_ref, of_ref, on_ref, wot_ref, g_ref, o_ref):
    heads, n_tiles, hd, t = of_ref.shape[1:]
    for c in range(n_tiles):
        of = of_ref[0, :, c].reshape(heads * hd, t)
        on = on_ref[0, :, c].reshape(heads * hd, t)
        mix_t = _dot(wot_ref[:, 0:FOX_QKV], of) + _dot(wot_ref[:, FOX_QKV:], on)
        rows = slice(c * t, (c + 1) * t)
        o_ref[0, rows, :] = x_ref[0, rows, :] + _rms(mix_t.T, g_ref[...])


def _mix_out(x, o_fox, o_nsa, wot, g, tm=1024):
    B, T, D = x.shape
    heads, _, hd, t = o_fox.shape[1:]
    row = pl.BlockSpec((1, tm, D), lambda b, i: (b, i, 0))
    tiles = pl.BlockSpec((1, heads, tm // t, hd, t), lambda b, i: (b, 0, i, 0, 0))
    return pl.pallas_call(
        _mix_out_kernel,
        grid=(B, T // tm),
        in_specs=[row, tiles, tiles,
                  pl.BlockSpec(wot.shape, lambda b, i: (0, 0)),
                  pl.BlockSpec((1, D), lambda b, i: (0, 0))],
        out_specs=row,
        out_shape=jax.ShapeDtypeStruct((B, T, D), F32),
        compiler_params=_params(("parallel", "parallel")),
        name="mix_out",
    )(x, o_fox, o_nsa, wot, g)


def _mem_kv_kernel(m_ref, g_ref, w_ref, k_ref, v_ref):
    m = _rms(m_ref[0], g_ref[...]).astype(BF16)
    k_ref[0] = _dot(m, w_ref[:, :D_MODEL]).astype(BF16)
    v_ref[0] = _dot(m, w_ref[:, D_MODEL:]).astype(BF16)


def _mem_kv(mem, g, wkv):
    B, M, D = mem.shape
    blk = pl.BlockSpec((1, M, D), lambda b: (b, 0, 0))
    return pl.pallas_call(
        _mem_kv_kernel,
        grid=(B,),
        in_specs=[blk, pl.BlockSpec((1, D), lambda b: (0, 0)), pl.BlockSpec(wkv.shape, lambda b: (0, 0))],
        out_specs=[blk, blk],
        out_shape=[jax.ShapeDtypeStruct((B, M, D), BF16)] * 2,
        compiler_params=_params(("parallel",)),
        name="mem_kv",
    )(mem, g, wkv)


def _cross_kernel(h_ref, k_ref, v_ref, wq_ref, wo_ref, gpre_ref, gpost_ref, o_ref):
    h = h_ref[0]
    n = _rms(h, gpre_ref[...]).astype(BF16)
    scale = CROSS_HEAD_DIM ** -0.5
    q_all = (_dot(n, wq_ref[...]) * scale).astype(BF16)
    heads = []
    for hh in range(CROSS_HEADS):
        sl = slice(hh * CROSS_HEAD_DIM, (hh + 1) * CROSS_HEAD_DIM)
        s = _dot_nt(q_all[:, sl], k_ref[0, :, sl])
        m = jnp.max(s, axis=-1, keepdims=True)
        p = jnp.exp(s - m)
        l = jnp.sum(p, axis=-1, keepdims=True)
        heads.append((_dot(p.astype(BF16), v_ref[0, :, sl]) / l).astype(BF16))
    out = _dot(jnp.concatenate(heads, axis=-1), wo_ref[...])
    o_ref[0] = h + _rms(out, gpost_ref[...])


def _cross(h, k, v, wq, wo, g_pre, g_post, tm=1024):
    B, T, D = h.shape
    M = k.shape[1]
    row = pl.BlockSpec((1, tm, D), lambda b, i: (b, i, 0))
    kvb = pl.BlockSpec((1, M, D), lambda b, i: (b, 0, 0))
    wsp = pl.BlockSpec((D, D), lambda b, i: (0, 0))
    gsp = pl.BlockSpec((1, D), lambda b, i: (0, 0))
    return pl.pallas_call(
        _cross_kernel,
        grid=(B, T // tm),
        in_specs=[row, kvb, kvb, wsp, wsp, gsp, gsp],
        out_specs=row,
        out_shape=jax.ShapeDtypeStruct((B, T, D), F32),
        compiler_params=_params(("parallel", "parallel")),
        name="mem_cross",
    )(h, k, v, wq, wo, g_pre, g_post)


def _mlp_kernel(h_ref, wu_ref, wd_ref, gpre_ref, gpost_ref, o_ref, *, hc):
    h = h_ref[...]
    n = _rms(h, gpre_ref[...]).astype(BF16)
    acc = jnp.zeros(h.shape, F32)
    for c in range(wu_ref.shape[1] // hc):
        u = jnp.maximum(_dot(n, wu_ref[:, c * hc:(c + 1) * hc]), 0.0)
        acc = acc + _dot((u * u).astype(BF16), wd_ref[c * hc:(c + 1) * hc, :])
    o_ref[...] = h + _rms(acc, gpost_ref[...])


def _mlp(h, wu, wd, g_pre, g_post, tm=512, hc=512):
    N, D = h.shape
    row = pl.BlockSpec((tm, D), lambda i: (i, 0))
    gsp = pl.BlockSpec((1, D), lambda i: (0, 0))
    once = pl.Buffered(1)
    return pl.pallas_call(
        functools.partial(_mlp_kernel, hc=hc),
        grid=(N // tm,),
        in_specs=[row,
                  pl.BlockSpec(wu.shape, lambda i: (0, 0), pipeline_mode=once),
                  pl.BlockSpec(wd.shape, lambda i: (0, 0), pipeline_mode=once),
                  gsp, gsp],
        out_specs=row,
        out_shape=jax.ShapeDtypeStruct((N, D), F32),
        compiler_params=_params(("parallel",)),
        name="relu2_mlp",
    )(h, wu, wd, g_pre, g_post)


def _layer(h, mem, g_mix_pre, w_in, b_forget, w_ck1, w_ck2, w_cv1, w_cv2, pe_k, pe_v,
           w_mix_out, g_mix_post, g_x_pre, g_mem, w_xq, w_xkv, w_xo, g_x_post,
           g_mlp_pre, w_up, w_down, g_mlp_post):
    B, T, D = h.shape
    H, G, hd = NSA_HEADS, NSA_KV_HEADS, HEAD_DIM
    row = lambda g: g.reshape(1, -1)

    cols = {}
    lo = 0
    for name, size in (("fq", FOX_QKV), ("fk", FOX_QKV), ("fv", FOX_QKV), ("ff", FOX_HEADS), ("nq", NSA_Q),
                       ("kc", NSA_KV), ("vc", NSA_KV), ("ks", NSA_KV), ("vs", NSA_KV), ("kw", NSA_KV),
                       ("vw", NSA_KV), ("ng", NSA_HEADS * N_BRANCH)):
        cols[name] = w_in[:, lo:lo + size]
        lo += size
    ng_branch_major = cols["ng"].reshape(D, NSA_HEADS, N_BRANCH).transpose(0, 2, 1).reshape(D, -1)
    wt = jnp.concatenate([cols[k] for k in ("fq", "fv", "nq", "vs", "vw", "ff")]
                         + [ng_branch_major], axis=1).T.astype(BF16)
    wk = jnp.concatenate([cols[k] for k in ("fk", "kc", "vc", "ks", "kw")], axis=1).astype(BF16)
    half = hd // 2
    inv = ROPE_THETA ** (-jnp.arange(half, dtype=F32) / half)
    ang = inv[:, None] * jnp.arange(T, dtype=F32)[None, :]
    cos_t, sin_t = jnp.cos(ang), jnp.sin(ang)
    cos_k, sin_k = jnp.tile(cos_t.T, (1, 2 * G)), jnp.tile(sin_t.T, (1, 2 * G))

    def slot_placement(heads):
        src = jnp.arange(heads * hd)
        return (jnp.arange(heads * AUG)[None, :] == ((src // hd) * AUG + src % hd)[:, None]).astype(F32)

    d = jnp.arange(G * hd)
    partner = jnp.where(d % hd < half, d + half, d - half)
    rot = (jnp.arange(G * hd)[:, None] == partner[None, :]) * jnp.where(d % hd < half, -1.0, 1.0)[None, :]
    place_rope = jnp.concatenate([slot_placement(G), rot @ slot_placement(G)], axis=0).astype(BF16)
    place_fox = slot_placement(FOX_HEADS).astype(BF16)

    t = 256
    hb = 8
    nq = T // t
    (fq, k_fox, fv, nq_t, nqr, kcvc, ka_slc, vs, ka_win, vw, lf_t, gt_t, gate_tiles) = _in_proj(
        h, row(g_mix_pre), wt, wk, b_forget.reshape(FOX_HEADS, 1), cos_t, sin_t, cos_k, sin_k,
        place_fox, place_rope, t)
    gate_h = gate_tiles.reshape(B, N_BRANCH, H, nq, 1, t)

    aug_fox, ka_fox = _decay(lf_t, k_fox, t)
    one_gate = jnp.ones((B, FOX_HEADS, nq, 1, t), F32)
    o_fox = _flash(fq, ka_fox, fv, one_gate, aug_fox, hb=hb)

    hsz = CMP_STRIDE * hd
    w1cat = jnp.stack([jnp.concatenate([w[:hsz], w[hsz:]], axis=1) for w in (w_ck1, w_cv1)]).astype(BF16)
    w1exp = jnp.zeros((2, CMP_STRIDE, G, hd, G, 2, CMP_HIDDEN), F32)
    for kind, w in enumerate((w_ck1, w_cv1)):
        halves = w.reshape(2, CMP_STRIDE, hd, CMP_HIDDEN).transpose(1, 2, 0, 3)
        for g in range(G):
            w1exp = w1exp.at[kind, :, g, :, g].set(halves)
    w1exp = w1exp.reshape(2, CMP_STRIDE, G * hd, G * 2 * CMP_HIDDEN).astype(BF16)
    pe8 = jnp.stack([jnp.broadcast_to(p.reshape(1, -1), (8, CMP_LEN * hd)) for p in (pe_k, pe_v)])
    kc, vct = _compress(kcvc, w1exp, pe8, w1cat, w_ck2.astype(BF16), w_cv2.T.astype(BF16))

    gates = gt_t.reshape(B, N_BRANCH, G, NSA_REP, T)
    n_sel = T // SEL_BLOCK
    o_cmp, selbias = _cmp_attn(nq_t, kc, vct, gates[:, 0:1], n_sel, t)

    o_slc = _flash(nqr, ka_slc, vs, gate_h[:, 1], selbias, hb=hb)
    o_nsa = _flash(nqr, ka_win, vw, gate_h[:, 2], extras=(o_cmp, o_slc), hb=hb, window=WINDOW)

    h = _mix_out(h, o_fox, o_nsa, w_mix_out.T.astype(BF16), row(g_mix_post))

    k_mem, v_mem = _mem_kv(mem, row(g_mem), w_xkv.astype(BF16))
    h = _cross(h, k_mem, v_mem, w_xq.astype(BF16), w_xo.astype(BF16), row(g_x_pre), row(g_x_post))

    h = _mlp(h.reshape(B * T, D), w_up.astype(BF16), w_down.astype(BF16),
             row(g_mlp_pre), row(g_mlp_post)).reshape(B, T, D)
    return h


def kernel(x, mem, g_mix_pre, w_in, b_forget, w_ck1, w_ck2, w_cv1, w_cv2, pe_k, pe_v, w_mix_out, g_mix_post,
           g_x_pre, g_mem, w_xq, w_xkv, w_xo, g_x_post, g_mlp_pre, w_up, w_down, g_mlp_post):
    h = x
    for l in range(g_mix_pre.shape[0]):
        h = _layer(h, mem, g_mix_pre[l], w_in[l], b_forget[l], w_ck1[l], w_ck2[l], w_cv1[l], w_cv2[l],
                   pe_k[l], pe_v[l], w_mix_out[l], g_mix_post[l], g_x_pre[l], g_mem[l], w_xq[l], w_xkv[l],
                   w_xo[l], g_x_post[l], g_mlp_pre[l], w_up[l], w_down[l], g_mlp_post[l])
    return h
```

```python
import functools
import math

import jax
import jax.numpy as jnp
from jax import lax
from jax.experimental import pallas as pl
from jax.experimental.pallas import tpu as pltpu

D_MODEL = 1024
HEAD_DIM = 64
FOX_HEADS = 8
NSA_HEADS = 8
NSA_KV_HEADS = 2
NSA_REP = NSA_HEADS // NSA_KV_HEADS
CMP_LEN = 32
CMP_STRIDE = 16
CMP_HIDDEN = 2 * HEAD_DIM
SEL_BLOCK = 64
SEL_TOPK = 16
WINDOW = 512
N_BRANCH = 3
CROSS_HEADS = 4
CROSS_HEAD_DIM = D_MODEL // CROSS_HEADS
MLP_HIDDEN = 4 * D_MODEL
ROPE_THETA = 10000.0
RMS_EPS = 1e-6
FORCE_SCORE = 1e4
MASKED = -1e30
LOG2E = math.log2(math.e)

FOX_QKV = FOX_HEADS * HEAD_DIM
NSA_Q = NSA_HEADS * HEAD_DIM
NSA_KV = NSA_KV_HEADS * HEAD_DIM
AUG = 128

V7X_VMEM_LIMIT = 56 * 1024 * 1024

F32 = jnp.float32
BF16 = jnp.bfloat16


def _params(sem, vmem=V7X_VMEM_LIMIT):
    return pltpu.CompilerParams(dimension_semantics=sem, vmem_limit_bytes=vmem)


def _rms(x, g):
    return x * lax.rsqrt(jnp.mean(x * x, axis=-1, keepdims=True) + RMS_EPS) * g


def _dot(a, b):
    return jnp.dot(a, b, preferred_element_type=F32)


def _dot_nt(a, b):
    return lax.dot_general(a, b, (((1,), (1,)), ((), ())), preferred_element_type=F32)


def _split3(x):
    hi = x.astype(BF16)
    r1 = x - hi.astype(F32)
    mid = r1.astype(BF16)
    lo = (r1 - mid.astype(F32)).astype(BF16)
    return hi, mid, lo


_R_FQ, _R_FV, _R_NQ, _R_VS, _R_VW, _R_FF, _R_NG, _R_END = 0, 512, 1024, 1536, 1664, 1792, 1800, 1824
_C_FK, _C_KC, _C_KS, _C_KW, _C_END = 0, 512, 768, 896, 1024
V_ROWS = HEAD_DIM + 16


def _in_proj_kernel(x_ref, g_ref, wt_ref, wk_ref, bf_ref, cos_ref, sin_ref, cosk_ref, sink_ref, pfox_ref, prope_ref,
                    fq_ref, kfox_ref, fv_ref, nq_ref, nqr_ref, kcvc_ref,
                    kslc_ref, vs_ref, kwin_ref, vw_ref, lf_ref, gt_ref, gtt_ref, *, t):
    n = _rms(x_ref[0], g_ref[...]).astype(BF16)
    tm = n.shape[0]
    cos = cos_ref[...]
    sin = sin_ref[...]
    half = HEAD_DIM // 2

    def proj(lo, hi):
        return _dot_nt(wt_ref[lo:hi, :], n)

    def rope(r):
        parts = []
        for h in range(r.shape[0] // HEAD_DIM):
            x1 = r[h * HEAD_DIM:h * HEAD_DIM + half]
            x2 = r[h * HEAD_DIM + half:(h + 1) * HEAD_DIM]
            parts += [x1 * cos - x2 * sin, x2 * cos + x1 * sin]
        return jnp.concatenate(parts, axis=0)

    def store_tiles(r, out_ref):
        r = r.astype(BF16)
        rows = out_ref.shape[3]
        if rows > HEAD_DIM:
            sub = lax.broadcasted_iota(jnp.int32, (rows - HEAD_DIM, t), 0)
            tail = jnp.where(sub == 0, 1.0, 0.0).astype(BF16)
        for h in range(r.shape[0] // HEAD_DIM):
            for c in range(tm // t):
                tile = r[h * HEAD_DIM:(h + 1) * HEAD_DIM, c * t:(c + 1) * t]
                out_ref[0, h, c] = tile if rows == HEAD_DIM else jnp.concatenate([tile, tail], axis=0)

    scale = HEAD_DIM ** -0.5 * LOG2E
    store_tiles(proj(_R_FQ, _R_FV) * scale, fq_ref)
    store_tiles(proj(_R_FV, _R_NQ), fv_ref)
    nq = proj(_R_NQ, _R_VS) * scale
    nq_ref[0] = nq.astype(BF16)
    store_tiles(rope(nq), nqr_ref)
    store_tiles(proj(_R_VS, _R_VW), vs_ref)
    store_tiles(proj(_R_VW, _R_FF), vw_ref)
    small = proj(_R_FF, _R_END)
    z = small[0:FOX_HEADS] + bf_ref[...]
    lf_ref[0] = jnp.minimum(z, 0.0) - jnp.log1p(jnp.exp(-jnp.abs(z)))
    gates = jax.nn.sigmoid(small[FOX_HEADS:])
    gt_ref[0] = gates
    for r in range(gates.shape[0]):
        for c in range(tm // t):
            gtt_ref[0, r, c] = gates[r:r + 1, c * t:(c + 1) * t]

    tok = _dot(n, wk_ref[...])
    kfox_ref[0] = _dot(tok[:, _C_FK:_C_KC].astype(BF16), pfox_ref[...]).astype(BF16)
    kcvc_ref[0, 0] = tok[:, _C_KC:_C_KC + NSA_KV]
    kcvc_ref[0, 1] = tok[:, _C_KC + NSA_KV:_C_KS]

    def rope_placed(k):
        both = jnp.concatenate([(k * cosk_ref[...]).astype(BF16), (k * sink_ref[...]).astype(BF16)], axis=1)
        return _dot(both, prope_ref[...])

    lane = lax.broadcasted_iota(jnp.int32, (tm, NSA_KV_HEADS * AUG), 1) & (AUG - 1)
    pos = pl.program_id(1) * tm + lax.broadcasted_iota(jnp.int32, (tm, NSA_KV_HEADS * AUG), 0)
    block_col = lane == HEAD_DIM + jnp.right_shift(pos, SEL_BLOCK.bit_length() - 1)
    kslc_ref[0] = (rope_placed(tok[:, _C_KS:_C_KW]) + jnp.where(block_col, 1.0, 0.0)).astype(BF16)
    kwin_ref[0] = rope_placed(tok[:, _C_KW:_C_END]).astype(BF16)


def _in_proj(x, g, wt, wk, b_forget, cos_t, sin_t, cos_k, sin_k, place_fox, place_rope, t, tm=1024):
    B, T, D = x.shape
    grid = (B, T // tm)
    chan = lambda c, dt: (jax.ShapeDtypeStruct((B, c, T), dt), pl.BlockSpec((1, c, tm), lambda b, i: (b, 0, i)))
    toks = lambda c: (jax.ShapeDtypeStruct((B, T, c), BF16), pl.BlockSpec((1, tm, c), lambda b, i: (b, i, 0)))
    tiles = lambda h, rows, dt=BF16: (jax.ShapeDtypeStruct((B, h, T // t, rows, t), dt),
                                      pl.BlockSpec((1, h, tm // t, rows, t), lambda b, i: (b, 0, i, 0, 0)))
    n_gate = NSA_HEADS * N_BRANCH
    outs = [tiles(FOX_HEADS, HEAD_DIM), toks(FOX_HEADS * AUG), tiles(FOX_HEADS, V_ROWS), chan(NSA_Q, BF16),
            tiles(NSA_HEADS, HEAD_DIM),
            (jax.ShapeDtypeStruct((B, 2, T, NSA_KV), F32), pl.BlockSpec((1, 2, tm, NSA_KV), lambda b, i: (b, 0, i, 0))),
            toks(NSA_KV_HEADS * AUG), tiles(NSA_KV_HEADS, V_ROWS), toks(NSA_KV_HEADS * AUG),
            tiles(NSA_KV_HEADS, V_ROWS), chan(FOX_HEADS, F32), chan(n_gate, F32), tiles(n_gate, 1, F32)]
    full = lambda a: pl.BlockSpec(a.shape, lambda b, i: (0,) * a.ndim)
    return pl.pallas_call(
        functools.partial(_in_proj_kernel, t=t),
        grid=grid,
        in_specs=[
            pl.BlockSpec((1, tm, D), lambda b, i: (b, i, 0)),
            full(g), full(wt), full(wk), full(b_forget),
            pl.BlockSpec((HEAD_DIM // 2, tm), lambda b, i: (0, i)),
            pl.BlockSpec((HEAD_DIM // 2, tm), lambda b, i: (0, i)),
            pl.BlockSpec((tm, NSA_KV), lambda b, i: (i, 0)),
            pl.BlockSpec((tm, NSA_KV), lambda b, i: (i, 0)),
            full(place_fox), full(place_rope),
        ],
        out_specs=[o[1] for o in outs],
        out_shape=[o[0] for o in outs],
        compiler_params=_params(("parallel", "parallel")),
        name="in_proj",
    )(x, g, wt, wk, b_forget, cos_t, sin_t, cos_k, sin_k, place_fox, place_rope)


_CS = 128
Q_AUG_FOX = 16


def _decay_kernel(x_ref, k_ref, qa_ref, ka_ref, *, t):
    x = x_ref[0]
    H, T = x.shape
    r = lax.broadcasted_iota(jnp.int32, (_CS, _CS), 0)
    c = lax.broadcasted_iota(jnp.int32, (_CS, _CS), 1)
    tri = (r <= c).astype(BF16)
    carry = jnp.zeros((H, 1), F32)
    chunks = []
    for ch in range(T // _CS):
        xh, xm, xl = _split3(x[:, ch * _CS:(ch + 1) * _CS])
        y = _dot(xh, tri) + _dot(xm, tri) + _dot(xl, tri) + carry
        carry = y[:, _CS - 1:_CS]
        chunks.append(y)
    cs = jnp.concatenate(chunks, axis=1) * LOG2E
    hi, mid, lo = _split3(cs)
    parts = jnp.concatenate([hi.astype(F32), mid.astype(F32), lo.astype(F32), jnp.zeros((H, T), F32)], axis=0)

    n_q = H * Q_AUG_FOX
    row = lax.broadcasted_iota(jnp.int32, (n_q, 4 * H), 0)
    src = lax.broadcasted_iota(jnp.int32, (n_q, 4 * H), 1)
    sh = Q_AUG_FOX.bit_length() - 1
    place_q = ((src == (row & (Q_AUG_FOX - 1)) * H + jnp.right_shift(row, sh)) & ((row & (Q_AUG_FOX - 1)) < 3))
    rq = lax.broadcasted_iota(jnp.int32, (n_q, T), 0) & (Q_AUG_FOX - 1)
    qa = _dot(place_q.astype(BF16), parts.astype(BF16)) + jnp.where((rq >= 3) & (rq < 6), 1.0, 0.0)
    qa = qa.astype(BF16)
    for h in range(H):
        for i in range(T // t):
            qa_ref[0, h, i] = qa[h * Q_AUG_FOX:(h + 1) * Q_AUG_FOX, i * t:(i + 1) * t]

    n_k = H * AUG
    src = lax.broadcasted_iota(jnp.int32, (4 * H, n_k), 0)
    col = lax.broadcasted_iota(jnp.int32, (4 * H, n_k), 1)
    ck = (col & (AUG - 1)) - HEAD_DIM
    hk = jnp.right_shift(col, AUG.bit_length() - 1)
    place_k = jnp.where((src == (ck - 3) * H + hk) & (ck >= 3) & (ck < 6), -1.0, 0.0).astype(BF16)
    ckt = (lax.broadcasted_iota(jnp.int32, (t, n_k), 1) & (AUG - 1)) - HEAD_DIM
    ones_k = jnp.where((ckt >= 0) & (ckt < 3), 1.0, 0.0)
    parts_t = parts.T.astype(BF16)
    for i in range(T // t):
        rows = slice(i * t, (i + 1) * t)
        ka_ref[0, rows, :] = (k_ref[0, rows, :].astype(F32) + _dot(parts_t[rows], place_k) + ones_k).astype(BF16)


def _decay(lf, k_placed, t):
    B, H, T = lf.shape
    kspec = pl.BlockSpec((1, T, H * AUG), lambda b: (b, 0, 0))
    return pl.pallas_call(
        functools.partial(_decay_kernel, t=t),
        grid=(B,),
        in_specs=[pl.BlockSpec((1, H, T), lambda b: (b, 0, 0)), kspec],
        out_specs=[pl.BlockSpec((1, H, T // t, Q_AUG_FOX, t), lambda b: (b, 0, 0, 0, 0)), kspec],
        out_shape=[jax.ShapeDtypeStruct((B, H, T // t, Q_AUG_FOX, t), BF16),
                   jax.ShapeDtypeStruct(k_placed.shape, BF16)],
        input_output_aliases={1: 1},
        compiler_params=_params(("parallel",)),
        name="decay_cumsum",
    )(lf, k_placed)


def _flash_kernel(q_ref, k_ref, v_ref, mask_ref, g_ref, *rest, t, back, n_aug, n_extra):
    aug_ref = rest[0] if n_aug else None
    extras = rest[n_aug:n_aug + n_extra]
    o_ref, s_ref, p_ref, acc_ref = rest[n_aug + n_extra:]
    hb, nq, hd = q_ref.shape[1], q_ref.shape[2], q_ref.shape[3]
    rep = hb // v_ref.shape[1]
    n_pairs = sum(min(i, back) + 1 for i in range(nq))

    def masked_scores(h, i, j):
        rows = [q_ref[0, h, i]]
        if aug_ref is not None:
            rows.append(aug_ref[0, h if aug_ref.shape[1] == hb else h // rep, i])
        pad = AUG - sum(r.shape[0] for r in rows)
        q = jnp.concatenate(rows + [jnp.zeros((pad, t), BF16)], axis=0)
        kh = h // rep
        k = k_ref[0, pl.ds(pl.multiple_of(j * t, t), t), kh * AUG:(kh + 1) * AUG]
        s = _dot(k, q) + mask_ref[jnp.maximum(j - i + (mask_ref.shape[0] - 1), 0)]
        s_ref[h] = s
        return jnp.max(s, axis=0, keepdims=True)

    def weighted_values(h, j):
        return _dot(v_ref[0, h // rep, j], p_ref[h])

    def emit(h, i, acc):
        l = acc[hd:hd + 1]
        o_ref[0, h, i] = (acc[:hd] * jnp.where(l > 0.0, g_ref[0, h, i] / l, 0.0)).astype(o_ref.dtype)

    tile_max = []
    for h in range(hb):
        tile_max.append(masked_scores(h, 0, 0))
        p_ref[h] = jnp.zeros((t, t), BF16)
        acc_ref[h] = jnp.zeros(acc_ref.shape[1:], F32)

    def body(_, carry):
        (i, j, i_prev, j_prev), heads = carry
        row_end = j == i
        i_next = jnp.minimum(jnp.where(row_end, i + 1, i), nq - 1)
        j_next = jnp.where(row_end, jnp.maximum(i_next - back, 0), j + 1)
        row_start = j == jnp.maximum(i - back, 0)
        out = []
        for h in range(hb):
            m, alpha, s_max = heads[h]
            acc = acc_ref[h] * alpha + weighted_values(h, j_prev)
            acc_ref[h] = acc
            emit(h, i_prev, acc)
            m = jnp.where(row_start, MASKED, m)
            m_new = jnp.maximum(m, s_max)
            alpha = jnp.exp2(m - m_new)
            p_ref[h] = jnp.exp2(s_ref[h] - m_new).astype(BF16)
            s_max = masked_scores(h, i_next, j_next)
            out.append((m_new, alpha, s_max))
        return (i_next, j_next, i, j), tuple(out)

    zero = jnp.int32(0)
    heads = tuple((jnp.full((1, t), MASKED, F32), jnp.ones((1, t), F32), tile_max[h]) for h in range(hb))
    _, heads = lax.fori_loop(0, n_pairs, body, ((zero, zero, zero, zero), heads))
    for h in range(hb):
        m, alpha, _ = heads[h]
        emit(h, nq - 1, acc_ref[h] * alpha + weighted_values(h, nq - 1))
    for x_ref in extras:
        for h in range(hb):
            for i in range(nq):
                o_ref[0, h, i] = (o_ref[0, h, i].astype(F32) + x_ref[0, h, i].astype(F32)).astype(o_ref.dtype)


def _mask_table(t, n, window):
    k = jnp.arange(t)[:, None]
    q = jnp.arange(t)[None, :]
    tiles = []
    for d in range(n):
        off = (d - (n - 1)) * t
        ok = (k + off) <= q
        if window is not None:
            ok = ok & ((k + off) > q - window)
        tiles.append(ok)
    return jnp.where(jnp.stack(tiles), 0.0, MASKED).astype(F32)


def _flash(q, ka, v_t, gate, aug=None, extras=(), *, hb, window=None):
    B, H, nq, hd, t = q.shape
    G = v_t.shape[1]
    T = nq * t
    assert hb % (H // G) == 0
    kb = hb // (H // G)
    if window is None:
        back = nq - 1
        masks = _mask_table(t, 2, None)
    else:
        assert window % t == 0
        back = window // t
        masks = _mask_table(t, back + 1, window)
    tile = lambda n, c: pl.BlockSpec((1, n, nq, c, t), lambda b, h: (b, h, 0, 0, 0))
    augs = () if aug is None else (aug,)
    aug_specs = [tile(hb if a.shape[1] == H else kb, a.shape[3]) for a in augs]
    return pl.pallas_call(
        functools.partial(_flash_kernel, t=t, back=back, n_aug=len(augs), n_extra=len(extras)),
        grid=(B, H // hb),
        in_specs=[
            tile(hb, hd),
            pl.BlockSpec((1, T, kb * AUG), lambda b, h: (b, 0, h)),
            tile(kb, v_t.shape[3]),
            pl.BlockSpec(masks.shape, lambda b, h: (0, 0, 0)),
            tile(hb, 1),
        ] + aug_specs + [tile(hb, hd)] * len(extras),
        out_specs=tile(hb, hd),
        out_shape=jax.ShapeDtypeStruct((B, H, nq, hd, t), BF16),
        scratch_shapes=[pltpu.VMEM((hb, t, t), F32), pltpu.VMEM((hb, t, t), BF16),
                        pltpu.VMEM((hb, v_t.shape[3], t), F32)],
        compiler_params=_params(("parallel", "parallel")),
        name="flash_window" if window is not None else "flash_causal",
    )(q, ka, v_t, masks, gate, *augs, *extras)


def _compress_kernel(x_ref, w1_ref, pe_ref, w1p_ref, w2k_ref, w2vt_ref, kc_ref, vct_ref):
    half = CMP_STRIDE * HEAD_DIM
    n_chunk = x_ref.shape[2] // CMP_STRIDE
    for kind in range(2):
        w1p = w1p_ref[kind]
        pe = pe_ref[kind].astype(BF16)
        bias = (_dot(pe[:, :half], w1p[:, :CMP_HIDDEN]) + _dot(pe[:, half:], w1p[:, CMP_HIDDEN:]))[0:1]
        ab = jnp.zeros((n_chunk, 2 * NSA_KV_HEADS * CMP_HIDDEN), F32)
        for p in range(CMP_STRIDE):
            x_p = x_ref[0, kind, pl.ds(p, n_chunk, stride=CMP_STRIDE), :].astype(BF16)
            ab = ab + _dot(x_p, w1_ref[kind, p])
        for g in range(NSA_KV_HEADS):
            lo = g * 2 * CMP_HIDDEN
            nxt = pltpu.roll(ab[:, lo + CMP_HIDDEN:lo + 2 * CMP_HIDDEN], n_chunk - 1, 0)
            pre = ab[:, lo:lo + CMP_HIDDEN] + nxt + bias
            hdn = (pre * jax.nn.sigmoid(pre)).astype(BF16)
            if kind == 0:
                kc_ref[0, g] = _dot(hdn, w2k_ref[...]).astype(BF16)
            else:
                vct_ref[0, g] = _dot_nt(w2vt_ref[...], hdn).astype(BF16)


def _compress(xc, w1exp, pe8, w1cat, w2k, w2vt):
    B, _, T, _ = xc.shape
    n_chunk = T // CMP_STRIDE
    full = lambda a: pl.BlockSpec(a.shape, lambda b: (0,) * a.ndim)
    return pl.pallas_call(
        _compress_kernel,
        grid=(B,),
        in_specs=[pl.BlockSpec((1,) + xc.shape[1:], lambda b: (b, 0, 0, 0)),
                  full(w1exp), full(pe8), full(w1cat), full(w2k), full(w2vt)],
        out_specs=[pl.BlockSpec((1, NSA_KV_HEADS, n_chunk, HEAD_DIM), lambda b: (b, 0, 0, 0)),
                   pl.BlockSpec((1, NSA_KV_HEADS, HEAD_DIM, n_chunk), lambda b: (b, 0, 0, 0))],
        out_shape=[jax.ShapeDtypeStruct((B, NSA_KV_HEADS, n_chunk, HEAD_DIM), BF16),
                   jax.ShapeDtypeStruct((B, NSA_KV_HEADS, HEAD_DIM, n_chunk), BF16)],
        compiler_params=_params(("parallel",)),
        name="nsa_compress",
    )(xc, w1exp, pe8, w1cat, w2k, w2vt)


def _cmp_attn_kernel(q_ref, kc_ref, vct_ref, g_ref, o_ref, sb_ref, *, tq, t):
    i = pl.program_id(2)
    n_cmp = kc_ref.shape[2]
    n_sel = sb_ref.shape[3]
    n_io = lax.broadcasted_iota(jnp.int32, (n_cmp, tq), 0)
    t_io = i * tq + lax.broadcasted_iota(jnp.int32, (n_cmp, tq), 1)
    mask = jnp.where(n_io * CMP_STRIDE + (CMP_LEN - 1) <= t_io, 0.0, MASKED)
    any_valid = t_io[0:1] >= CMP_LEN - 1
    kc = kc_ref[0, 0]
    vct = vct_ref[0, 0]
    psum = jnp.zeros((n_cmp, tq), F32)
    for r in range(NSA_REP):
        q = q_ref[0, r * HEAD_DIM:(r + 1) * HEAD_DIM, :]
        s = _dot(kc, q) + mask
        m = jnp.max(s, axis=0, keepdims=True)
        e = jnp.exp2(s - m)
        l = jnp.sum(e, axis=0, keepdims=True)
        p = e * jnp.where(any_valid, 1.0 / l, 0.0)
        psum = psum + p
        o = _dot(vct, p.astype(BF16))
        o = (o * g_ref[0, 0, 0, r:r + 1, :]).astype(BF16)
        for c in range(tq // t):
            o_ref[0, r, c] = o[:, c * t:(c + 1) * t]

    jr = lax.broadcasted_iota(jnp.int32, (n_sel, n_cmp), 0)
    nc = lax.broadcasted_iota(jnp.int32, (n_sel, n_cmp), 1)
    overlap = ((nc * CMP_STRIDE < (jr + 1) * SEL_BLOCK)
               & (nc * CMP_STRIDE + CMP_LEN > jr * SEL_BLOCK)).astype(BF16)
    ph, pm, pl_ = _split3(psum)
    imp = _dot(overlap, ph) + _dot(overlap, pm) + _dot(overlap, pl_)
    j_io = lax.broadcasted_iota(jnp.int32, (n_sel, tq), 0)
    cur = jnp.right_shift(i * tq + lax.broadcasted_iota(jnp.int32, (n_sel, tq), 1),
                          SEL_BLOCK.bit_length() - 1)
    is_cur = j_io == cur
    is_fixed = (j_io == 0) | (j_io == cur - 1)
    imp = jnp.where(is_cur, 2.0 * FORCE_SCORE, jnp.where(is_fixed, FORCE_SCORE, imp))
    imp = jnp.where(j_io <= cur, imp, -1.0)
    grp = 8
    imp_g = [imp[k * grp:(k + 1) * grp] for k in range(n_sel // grp)]
    sub = lax.broadcasted_iota(jnp.int32, (grp, tq), 0)
    cnt_g = [jnp.zeros((grp, tq), F32) for _ in imp_g]
    for jp in range(n_sel):
        row = imp[jp:jp + 1, :]
        for k, x in enumerate(imp_g):
            ge = jnp.where(row >= x, 1.0, 0.0)
            gt = jnp.where(row > x, 1.0, 0.0)
            if k > jp // grp:
                beats = ge
            elif k < jp // grp:
                beats = gt
            else:
                beats = jnp.where(sub > jp % grp, ge, gt)
            cnt_g[k] = cnt_g[k] + beats
    cnt = jnp.concatenate(cnt_g, axis=0)
    sb = jnp.where(cnt < min(SEL_TOPK, n_sel), 0.0, MASKED).astype(BF16)
    for c in range(tq // t):
        sb_ref[0, 0, c] = sb[:, c * t:(c + 1) * t]


def _cmp_attn(nq_t, kc, vct, gate_cmp, n_sel, t, tq=2048):
    B, _, T = nq_t.shape
    G = kc.shape[1]
    rows = NSA_REP * HEAD_DIM
    return pl.pallas_call(
        functools.partial(_cmp_attn_kernel, tq=tq, t=t),
        grid=(B, G, T // tq),
        in_specs=[
            pl.BlockSpec((1, rows, tq), lambda b, g, i: (b, g, i)),
            pl.BlockSpec((1, 1) + kc.shape[2:], lambda b, g, i: (b, g, 0, 0)),
            pl.BlockSpec((1, 1) + vct.shape[2:], lambda b, g, i: (b, g, 0, 0)),
            pl.BlockSpec((1, 1, 1, NSA_REP, tq), lambda b, g, i: (b, 0, g, 0, i)),
        ],
        out_specs=[pl.BlockSpec((1, NSA_REP, tq // t, HEAD_DIM, t), lambda b, g, i: (b, g, i, 0, 0)),
                   pl.BlockSpec((1, 1, tq // t, n_sel, t), lambda b, g, i: (b, g, i, 0, 0))],
        out_shape=[jax.ShapeDtypeStruct((B, G * NSA_REP, T // t, HEAD_DIM, t), BF16),
                   jax.ShapeDtypeStruct((B, G, T // t, n_sel, t), BF16)],
        compiler_params=_params(("parallel", "parallel", "parallel")),
        name="nsa_cmp_attn",
    )(nq_t, kc, vct, gate_cmp)


def _mix_out_kernel(x_ref, of_ref, on_ref, wot_ref, g_ref, o_ref):
    heads, n_tiles, hd, t = of_ref.shape[1:]
    for c in range(n_tiles):
        of = of_ref[0, :, c].reshape(heads * hd, t)
        on = on_ref[0, :, c].reshape(heads * hd, t)
        mix_t = _dot(wot_ref[:, 0:FOX_QKV], of) + _dot(wot_ref[:, FOX_QKV:], on)
        rows = slice(c * t, (c + 1) * t)
        o_ref[0, rows, :] = x_ref[0, rows, :] + _rms(mix_t.T, g_ref[...])


def _mix_out(x, o_fox, o_nsa, wot, g, tm=1024):
    B, T, D = x.shape
    heads, _, hd, t = o_fox.shape[1:]
    row = pl.BlockSpec((1, tm, D), lambda b, i: (b, i, 0))
    tiles = pl.BlockSpec((1, heads, tm // t, hd, t), lambda b, i: (b, 0, i, 0, 0))
    return pl.pallas_call(
        _mix_out_kernel,
        grid=(B, T // tm),
        in_specs=[row, tiles, tiles,
                  pl.BlockSpec(wot.shape, lambda b, i: (0, 0)),
                  pl.BlockSpec((1, D), lambda b, i: (0, 0))],
        out_specs=row,
        out_shape=jax.ShapeDtypeStruct((B, T, D), F32),
        compiler_params=_params(("parallel", "parallel")),
        name="mix_out",
    )(x, o_fox, o_nsa, wot, g)


def _mem_kv_kernel(m_ref, g_ref, w_ref, k_ref, v_ref):
    m = _rms(m_ref[0], g_ref[...]).astype(BF16)
    k_ref[0] = _dot(m, w_ref[:, :D_MODEL]).astype(BF16)
    v_ref[0] = _dot(m, w_ref[:, D_MODEL:]).astype(BF16)


def _mem_kv(mem, g, wkv):
    B, M, D = mem.shape
    blk = pl.BlockSpec((1, M, D), lambda b: (b, 0, 0))
    return pl.pallas_call(
        _mem_kv_kernel,
        grid=(B,),
        in_specs=[blk, pl.BlockSpec((1, D), lambda b: (0, 0)), pl.BlockSpec(wkv.shape, lambda b: (0, 0))],
        out_specs=[blk, blk],
        out_shape=[jax.ShapeDtypeStruct((B, M, D), BF16)] * 2,
        compiler_params=_params(("parallel",)),
        name="mem_kv",
    )(mem, g, wkv)


def _cross_kernel(h_ref, k_ref, v_ref, wq_ref, wo_ref, gpre_ref, gpost_ref, o_ref):
    h = h_ref[0]
    n = _rms(h, gpre_ref[...]).astype(BF16)
    scale = CROSS_HEAD_DIM ** -0.5
    q_all = (_dot(n, wq_ref[...]) * scale).astype(BF16)
    heads = []
    for hh in range(CROSS_HEADS):
        sl = slice(hh * CROSS_HEAD_DIM, (hh + 1) * CROSS_HEAD_DIM)
        s = _dot_nt(q_all[:, sl], k_ref[0, :, sl])
        m = jnp.max(s, axis=-1, keepdims=True)
        p = jnp.exp(s - m)
        l = jnp.sum(p, axis=-1, keepdims=True)
        heads.append((_dot(p.astype(BF16), v_ref[0, :, sl]) / l).astype(BF16))
    out = _dot(jnp.concatenate(heads, axis=-1), wo_ref[...])
    o_ref[0] = h + _rms(out, gpost_ref[...])


def _cross(h, k, v, wq, wo, g_pre, g_post, tm=1024):
    B, T, D = h.shape
    M = k.shape[1]
    row = pl.BlockSpec((1, tm, D), lambda b, i: (b, i, 0))
    kvb = pl.BlockSpec((1, M, D), lambda b, i: (b, 0, 0))
    wsp = pl.BlockSpec((D, D), lambda b, i: (0, 0))
    gsp = pl.BlockSpec((1, D), lambda b, i: (0, 0))
    return pl.pallas_call(
        _cross_kernel,
        grid=(B, T // tm),
        in_specs=[row, kvb, kvb, wsp, wsp, gsp, gsp],
        out_specs=row,
        out_shape=jax.ShapeDtypeStruct((B, T, D), F32),
        compiler_params=_params(("parallel", "parallel")),
        name="mem_cross",
    )(h, k, v, wq, wo, g_pre, g_post)


def _mlp_kernel(h_ref, wu_ref, wd_ref, gpre_ref, gpost_ref, o_ref, *, hc):
    h = h_ref[...]
    n = _rms(h, gpre_ref[...]).astype(BF16)
    acc = jnp.zeros(h.shape, F32)
    for c in range(wu_ref.shape[1] // hc):
        u = jnp.maximum(_dot(n, wu_ref[:, c * hc:(c + 1) * hc]), 0.0)
        acc = acc + _dot((u * u).astype(BF16), wd_ref[c * hc:(c + 1) * hc, :])
    o_ref[...] = h + _rms(acc, gpost_ref[...])


def _mlp(h, wu, wd, g_pre, g_post, tm=1024, hc=512):
    N, D = h.shape
    row = pl.BlockSpec((tm, D), lambda i: (i, 0))
    gsp = pl.BlockSpec((1, D), lambda i: (0, 0))
    once = pl.Buffered(1)
    return pl.pallas_call(
        functools.partial(_mlp_kernel, hc=hc),
        grid=(N // tm,),
        in_specs=[row,
                  pl.BlockSpec(wu.shape, lambda i: (0, 0), pipeline_mode=once),
                  pl.BlockSpec(wd.shape, lambda i: (0, 0), pipeline_mode=once),
                  gsp, gsp],
        out_specs=row,
        out_shape=jax.ShapeDtypeStruct((N, D), F32),
        compiler_params=_params(("parallel",)),
        name="relu2_mlp",
    )(h, wu, wd, g_pre, g_post)


def _layer(h, mem, g_mix_pre, w_in, b_forget, w_ck1, w_ck2, w_cv1, w_cv2, pe_k, pe_v,
           w_mix_out, g_mix_post, g_x_pre, g_mem, w_xq, w_xkv, w_xo, g_x_post,
           g_mlp_pre, w_up, w_down, g_mlp_post):
    B, T, D = h.shape
    H, G, hd = NSA_HEADS, NSA_KV_HEADS, HEAD_DIM
    row = lambda g: g.reshape(1, -1)

    cols = {}
    lo = 0
    for name, size in (("fq", FOX_QKV), ("fk", FOX_QKV), ("fv", FOX_QKV), ("ff", FOX_HEADS), ("nq", NSA_Q),
                       ("kc", NSA_KV), ("vc", NSA_KV), ("ks", NSA_KV), ("vs", NSA_KV), ("kw", NSA_KV),
                       ("vw", NSA_KV), ("ng", NSA_HEADS * N_BRANCH)):
        cols[name] = w_in[:, lo:lo + size]
        lo += size
    ng_branch_major = cols["ng"].reshape(D, NSA_HEADS, N_BRANCH).transpose(0, 2, 1).reshape(D, -1)
    wt = jnp.concatenate([cols[k] for k in ("fq", "fv", "nq", "vs", "vw", "ff")]
                         + [ng_branch_major], axis=1).T.astype(BF16)
    wk = jnp.concatenate([cols[k] for k in ("fk", "kc", "vc", "ks", "kw")], axis=1).astype(BF16)
    half = hd // 2
    inv = ROPE_THETA ** (-jnp.arange(half, dtype=F32) / half)
    ang = inv[:, None] * jnp.arange(T, dtype=F32)[None, :]
    cos_t, sin_t = jnp.cos(ang), jnp.sin(ang)
    cos_k, sin_k = jnp.tile(cos_t.T, (1, 2 * G)), jnp.tile(sin_t.T, (1, 2 * G))

    def slot_placement(heads):
        src = jnp.arange(heads * hd)
        return (jnp.arange(heads * AUG)[None, :] == ((src // hd) * AUG + src % hd)[:, None]).astype(F32)

    d = jnp.arange(G * hd)
    partner = jnp.where(d % hd < half, d + half, d - half)
    rot = (jnp.arange(G * hd)[:, None] == partner[None, :]) * jnp.where(d % hd < half, -1.0, 1.0)[None, :]
    place_rope = jnp.concatenate([slot_placement(G), rot @ slot_placement(G)], axis=0).astype(BF16)
    place_fox = slot_placement(FOX_HEADS).astype(BF16)

    t = 256
    hb = 8
    nq = T // t
    (fq, k_fox, fv, nq_t, nqr, kcvc, ka_slc, vs, ka_win, vw, lf_t, gt_t, gate_tiles) = _in_proj(
        h, row(g_mix_pre), wt, wk, b_forget.reshape(FOX_HEADS, 1), cos_t, sin_t, cos_k, sin_k,
        place_fox, place_rope, t)
    gate_h = gate_tiles.reshape(B, N_BRANCH, H, nq, 1, t)

    aug_fox, ka_fox = _decay(lf_t, k_fox, t)
    one_gate = jnp.ones((B, FOX_HEADS, nq, 1, t), F32)
    o_fox = _flash(fq, ka_fox, fv, one_gate, aug_fox, hb=hb)

    hsz = CMP_STRIDE * hd
    w1cat = jnp.stack([jnp.concatenate([w[:hsz], w[hsz:]], axis=1) for w in (w_ck1, w_cv1)]).astype(BF16)
    w1exp = jnp.zeros((2, CMP_STRIDE, G, hd, G, 2, CMP_HIDDEN), F32)
    for kind, w in enumerate((w_ck1, w_cv1)):
        halves = w.reshape(2, CMP_STRIDE, hd, CMP_HIDDEN).transpose(1, 2, 0, 3)
        for g in range(G):
            w1exp = w1exp.at[kind, :, g, :, g].set(halves)
    w1exp = w1exp.reshape(2, CMP_STRIDE, G * hd, G * 2 * CMP_HIDDEN).astype(BF16)
    pe8 = jnp.stack([jnp.broadcast_to(p.reshape(1, -1), (8, CMP_LEN * hd)) for p in (pe_k, pe_v)])
    kc, vct = _compress(kcvc, w1exp, pe8, w1cat, w_ck2.astype(BF16), w_cv2.T.astype(BF16))

    gates = gt_t.reshape(B, N_BRANCH, G, NSA_REP, T)
    n_sel = T // SEL_BLOCK
    o_cmp, selbias = _cmp_attn(nq_t, kc, vct, gates[:, 0:1], n_sel, t)

    o_slc = _flash(nqr, ka_slc, vs, gate_h[:, 1], selbias, hb=hb)
    o_nsa = _flash(nqr, ka_win, vw, gate_h[:, 2], extras=(o_cmp, o_slc), hb=hb, window=WINDOW)

    h = _mix_out(h, o_fox, o_nsa, w_mix_out.T.astype(BF16), row(g_mix_post))

    k_mem, v_mem = _mem_kv(mem, row(g_mem), w_xkv.astype(BF16))
    h = _cross(h, k_mem, v_mem, w_xq.astype(BF16), w_xo.astype(BF16), row(g_x_pre), row(g_x_post))

    h = _mlp(h.reshape(B * T, D), w_up.astype(BF16), w_down.astype(BF16),
             row(g_mlp_pre), row(g_mlp_post)).reshape(B, T, D)
    return h


def kernel(x, mem, g_mix_pre, w_in, b_forget, w_ck1, w_ck2, w_cv1, w_cv2, pe_k, pe_v, w_mix_out, g_mix_post,
           g_x_pre, g_mem, w_xq, w_xkv, w_xo, g_x_post, g_mlp_pre, w_up, w_down, g_mlp_post):
    h = x
    for l in range(g_mix_pre.shape[0]):
        h = _layer(h, mem, g_mix_pre[l], w_in[l], b_forget[l], w_ck1[l], w_ck2[l], w_cv1[l], w_cv2[l],
                   pe_k[l], pe_v[l], w_mix_out[l], g_mix_post[l], g_x_pre[l], g_mem[l], w_xq[l], w_xkv[l],
                   w_xo[l], g_x_post[l], g_mlp_pre[l], w_up[l], w_down[l], g_mlp_post[l])
    return h
```

```python
import functools
import math

import jax
import jax.numpy as jnp
from jax import lax
from jax.experimental import pallas as pl
from jax.experimental.pallas import tpu as pltpu

D_MODEL = 1024
HEAD_DIM = 64
FOX_HEADS = 8
NSA_HEADS = 8
NSA_KV_HEADS = 2
NSA_REP = NSA_HEADS // NSA_KV_HEADS
CMP_LEN = 32
CMP_STRIDE = 16
CMP_HIDDEN = 2 * HEAD_DIM
SEL_BLOCK = 64
SEL_TOPK = 16
WINDOW = 512
N_BRANCH = 3
CROSS_HEADS = 4
CROSS_HEAD_DIM = D_MODEL // CROSS_HEADS
MLP_HIDDEN = 4 * D_MODEL
ROPE_THETA = 10000.0
RMS_EPS = 1e-6
FORCE_SCORE = 1e4
MASKED = -1e30
LOG2E = math.log2(math.e)

FOX_QKV = FOX_HEADS * HEAD_DIM
NSA_Q = NSA_HEADS * HEAD_DIM
NSA_KV = NSA_KV_HEADS * HEAD_DIM
AUG = 128

V7X_VMEM_LIMIT = 56 * 1024 * 1024

F32 = jnp.float32
BF16 = jnp.bfloat16


def _params(sem, vmem=V7X_VMEM_LIMIT):
    return pltpu.CompilerParams(dimension_semantics=sem, vmem_limit_bytes=vmem)


def _rms(x, g):
    return x * lax.rsqrt(jnp.mean(x * x, axis=-1, keepdims=True) + RMS_EPS) * g


def _dot(a, b):
    return jnp.dot(a, b, preferred_element_type=F32)


def _dot_nt(a, b):
    return lax.dot_general(a, b, (((1,), (1,)), ((), ())), preferred_element_type=F32)


def _split3(x):
    hi = x.astype(BF16)
    r1 = x - hi.astype(F32)
    mid = r1.astype(BF16)
    lo = (r1 - mid.astype(F32)).astype(BF16)
    return hi, mid, lo


_R_FQ, _R_FV, _R_NQ, _R_VS, _R_VW, _R_FF, _R_NG, _R_END = 0, 512, 1024, 1536, 1664, 1792, 1800, 1824
_C_FK, _C_KC, _C_KS, _C_KW, _C_END = 0, 512, 768, 896, 1024
V_ROWS = HEAD_DIM + 16


def _in_proj_kernel(x_ref, g_ref, wt_ref, wk_ref, bf_ref, cos_ref, sin_ref, cosk_ref, sink_ref, prope_ref,
                    fq_ref, kfox_ref, fv_ref, nq_ref, nqr_ref, kcvc_ref,
                    kslc_ref, vs_ref, kwin_ref, vw_ref, lf_ref, gt_ref, gtt_ref, *, t):
    n = _rms(x_ref[0], g_ref[...]).astype(BF16)
    tm = n.shape[0]
    cos = cos_ref[...]
    sin = sin_ref[...]
    half = HEAD_DIM // 2

    def proj(lo, hi):
        return _dot_nt(wt_ref[lo:hi, :], n)

    def rope(r):
        parts = []
        for h in range(r.shape[0] // HEAD_DIM):
            x1 = r[h * HEAD_DIM:h * HEAD_DIM + half]
            x2 = r[h * HEAD_DIM + half:(h + 1) * HEAD_DIM]
            parts += [x1 * cos - x2 * sin, x2 * cos + x1 * sin]
        return jnp.concatenate(parts, axis=0)

    def store_tiles(r, out_ref):
        r = r.astype(BF16)
        rows = out_ref.shape[3]
        if rows > HEAD_DIM:
            sub = lax.broadcasted_iota(jnp.int32, (rows - HEAD_DIM, t), 0)
            tail = jnp.where(sub == 0, 1.0, 0.0).astype(BF16)
        for h in range(r.shape[0] // HEAD_DIM):
            for c in range(tm // t):
                tile = r[h * HEAD_DIM:(h + 1) * HEAD_DIM, c * t:(c + 1) * t]
                out_ref[0, h, c] = tile if rows == HEAD_DIM else jnp.concatenate([tile, tail], axis=0)

    scale = HEAD_DIM ** -0.5 * LOG2E
    store_tiles(proj(_R_FQ, _R_FV) * scale, fq_ref)
    store_tiles(proj(_R_FV, _R_NQ), fv_ref)
    nq = proj(_R_NQ, _R_VS) * scale
    nq_ref[0] = nq.astype(BF16)
    store_tiles(rope(nq), nqr_ref)
    store_tiles(proj(_R_VS, _R_VW), vs_ref)
    store_tiles(proj(_R_VW, _R_FF), vw_ref)
    small = proj(_R_FF, _R_END)
    z = small[0:FOX_HEADS] + bf_ref[...]
    lf_ref[0] = jnp.minimum(z, 0.0) - jnp.log1p(jnp.exp(-jnp.abs(z)))
    gates = jax.nn.sigmoid(small[FOX_HEADS:])
    gt_ref[0] = gates
    for r in range(gates.shape[0]):
        for c in range(tm // t):
            gtt_ref[0, r, c] = gates[r:r + 1, c * t:(c + 1) * t]

    tok = _dot(n, wk_ref[...])
    kfox_ref[0] = tok[:, _C_FK:_C_KC].astype(BF16)
    kcvc_ref[0, 0] = tok[:, _C_KC:_C_KC + NSA_KV]
    kcvc_ref[0, 1] = tok[:, _C_KC + NSA_KV:_C_KS]

    def rope_placed(k):
        both = jnp.concatenate([(k * cosk_ref[...]).astype(BF16), (k * sink_ref[...]).astype(BF16)], axis=1)
        return _dot(both, prope_ref[...])

    lane = lax.broadcasted_iota(jnp.int32, (tm, NSA_KV_HEADS * AUG), 1) & (AUG - 1)
    pos = pl.program_id(1) * tm + lax.broadcasted_iota(jnp.int32, (tm, NSA_KV_HEADS * AUG), 0)
    block_col = lane == HEAD_DIM + jnp.right_shift(pos, SEL_BLOCK.bit_length() - 1)
    kslc_ref[0] = (rope_placed(tok[:, _C_KS:_C_KW]) + jnp.where(block_col, 1.0, 0.0)).astype(BF16)
    kwin_ref[0] = rope_placed(tok[:, _C_KW:_C_END]).astype(BF16)


def _in_proj(x, g, wt, wk, b_forget, cos_t, sin_t, cos_k, sin_k, place_rope, t, tm=1024):
    B, T, D = x.shape
    grid = (B, T // tm)
    chan = lambda c, dt: (jax.ShapeDtypeStruct((B, c, T), dt), pl.BlockSpec((1, c, tm), lambda b, i: (b, 0, i)))
    toks = lambda c: (jax.ShapeDtypeStruct((B, T, c), BF16), pl.BlockSpec((1, tm, c), lambda b, i: (b, i, 0)))
    tiles = lambda h, rows, dt=BF16: (jax.ShapeDtypeStruct((B, h, T // t, rows, t), dt),
                                      pl.BlockSpec((1, h, tm // t, rows, t), lambda b, i: (b, 0, i, 0, 0)))
    n_gate = NSA_HEADS * N_BRANCH
    outs = [tiles(FOX_HEADS, HEAD_DIM), toks(FOX_QKV), tiles(FOX_HEADS, V_ROWS), chan(NSA_Q, BF16),
            tiles(NSA_HEADS, HEAD_DIM),
            (jax.ShapeDtypeStruct((B, 2, T, NSA_KV), F32), pl.BlockSpec((1, 2, tm, NSA_KV), lambda b, i: (b, 0, i, 0))),
            toks(NSA_KV_HEADS * AUG), tiles(NSA_KV_HEADS, V_ROWS), toks(NSA_KV_HEADS * AUG),
            tiles(NSA_KV_HEADS, V_ROWS), chan(FOX_HEADS, F32), chan(n_gate, F32), tiles(n_gate, 1, F32)]
    full = lambda a: pl.BlockSpec(a.shape, lambda b, i: (0,) * a.ndim)
    return pl.pallas_call(
        functools.partial(_in_proj_kernel, t=t),
        grid=grid,
        in_specs=[
            pl.BlockSpec((1, tm, D), lambda b, i: (b, i, 0)),
            full(g), full(wt), full(wk), full(b_forget),
            pl.BlockSpec((HEAD_DIM // 2, tm), lambda b, i: (0, i)),
            pl.BlockSpec((HEAD_DIM // 2, tm), lambda b, i: (0, i)),
            pl.BlockSpec((tm, NSA_KV), lambda b, i: (i, 0)),
            pl.BlockSpec((tm, NSA_KV), lambda b, i: (i, 0)),
            full(place_rope),
        ],
        out_specs=[o[1] for o in outs],
        out_shape=[o[0] for o in outs],
        compiler_params=_params(("parallel", "parallel")),
        name="in_proj",
    )(x, g, wt, wk, b_forget, cos_t, sin_t, cos_k, sin_k, place_rope)


_CS = 128
Q_AUG_FOX = 16


def _decay_kernel(x_ref, k_ref, qa_ref, ka_ref, *, t):
    x = x_ref[0]
    H, T = x.shape
    r = lax.broadcasted_iota(jnp.int32, (_CS, _CS), 0)
    c = lax.broadcasted_iota(jnp.int32, (_CS, _CS), 1)
    tri = (r <= c).astype(BF16)
    within = []
    for ch in range(T // _CS):
        xh, xm, xl = _split3(x[:, ch * _CS:(ch + 1) * _CS])
        within.append(_dot(xh, tri) + _dot(xm, tri) + _dot(xl, tri))
    carry = jnp.zeros((H, 1), F32)
    chunks = []
    for y in within:
        chunks.append(y + carry)
        carry = carry + y[:, _CS - 1:_CS]
    cs = jnp.concatenate(chunks, axis=1) * LOG2E
    hi, mid, lo = _split3(cs)
    parts = jnp.concatenate([hi.astype(F32), mid.astype(F32), lo.astype(F32), jnp.zeros((H, T), F32)], axis=0)

    n_q = H * Q_AUG_FOX
    row = lax.broadcasted_iota(jnp.int32, (n_q, 4 * H), 0)
    src = lax.broadcasted_iota(jnp.int32, (n_q, 4 * H), 1)
    sh = Q_AUG_FOX.bit_length() - 1
    place_q = ((src == (row & (Q_AUG_FOX - 1)) * H + jnp.right_shift(row, sh)) & ((row & (Q_AUG_FOX - 1)) < 3))
    rq = lax.broadcasted_iota(jnp.int32, (n_q, T), 0) & (Q_AUG_FOX - 1)
    qa = _dot(place_q.astype(BF16), parts.astype(BF16)) + jnp.where((rq >= 3) & (rq < 6), 1.0, 0.0)
    qa = qa.astype(BF16)
    for h in range(H):
        for i in range(T // t):
            qa_ref[0, h, i] = qa[h * Q_AUG_FOX:(h + 1) * Q_AUG_FOX, i * t:(i + 1) * t]

    n_k = H * HEAD_DIM
    src = lax.broadcasted_iota(jnp.int32, (4 * H, n_k), 0)
    col = lax.broadcasted_iota(jnp.int32, (4 * H, n_k), 1)
    ck = col & (HEAD_DIM - 1)
    hk = jnp.right_shift(col, HEAD_DIM.bit_length() - 1)
    place_k = jnp.where((src == (ck - 3) * H + hk) & (ck >= 3) & (ck < 6), -1.0, 0.0).astype(BF16)
    ckt = lax.broadcasted_iota(jnp.int32, (t, n_k), 1) & (HEAD_DIM - 1)
    ones_k = jnp.where(ckt < 3, 1.0, 0.0)
    parts_t = parts.T.astype(BF16)
    for i in range(T // t):
        rows = slice(i * t, (i + 1) * t)
        aug = (_dot(parts_t[rows], place_k) + ones_k).astype(BF16)
        k = k_ref[0, rows, :]
        ka_ref[0, rows, :] = jnp.concatenate(
            [piece for h in range(H) for piece in (k[:, h * HEAD_DIM:(h + 1) * HEAD_DIM],
                                                   aug[:, h * HEAD_DIM:(h + 1) * HEAD_DIM])], axis=1)


def _decay(lf, k, t):
    B, H, T = lf.shape
    return pl.pallas_call(
        functools.partial(_decay_kernel, t=t),
        grid=(B,),
        in_specs=[pl.BlockSpec((1, H, T), lambda b: (b, 0, 0)),
                  pl.BlockSpec((1, T, H * HEAD_DIM), lambda b: (b, 0, 0))],
        out_specs=[pl.BlockSpec((1, H, T // t, Q_AUG_FOX, t), lambda b: (b, 0, 0, 0, 0)),
                   pl.BlockSpec((1, T, H * AUG), lambda b: (b, 0, 0))],
        out_shape=[jax.ShapeDtypeStruct((B, H, T // t, Q_AUG_FOX, t), BF16),
                   jax.ShapeDtypeStruct((B, T, H * AUG), BF16)],
        compiler_params=_params(("parallel",)),
        name="decay_cumsum",
    )(lf, k)


def _flash_kernel(q_ref, k_ref, v_ref, mask_ref, g_ref, *rest, t, back, n_aug, n_extra):
    aug_ref = rest[0] if n_aug else None
    extras = rest[n_aug:n_aug + n_extra]
    o_ref, s_ref, p_ref, acc_ref = rest[n_aug + n_extra:]
    hb, nq, hd = q_ref.shape[1], q_ref.shape[2], q_ref.shape[3]
    rep = hb // v_ref.shape[1]
    n_pairs = sum(min(i, back) + 1 for i in range(nq))

    def masked_scores(h, i, j):
        rows = [q_ref[0, h, i]]
        if aug_ref is not None:
            rows.append(aug_ref[0, h if aug_ref.shape[1] == hb else h // rep, i])
        pad = AUG - sum(r.shape[0] for r in rows)
        q = jnp.concatenate(rows + [jnp.zeros((pad, t), BF16)], axis=0)
        kh = h // rep
        k = k_ref[0, pl.ds(pl.multiple_of(j * t, t), t), kh * AUG:(kh + 1) * AUG]
        s = _dot(k, q) + mask_ref[jnp.maximum(j - i + (mask_ref.shape[0] - 1), 0)]
        s_ref[h] = s
        return jnp.max(s, axis=0, keepdims=True)

    def weighted_values(h, j):
        return _dot(v_ref[0, h // rep, j], p_ref[h])

    def emit(h, i, acc):
        l = acc[hd:hd + 1]
        o_ref[0, h, i] = (acc[:hd] * jnp.where(l > 0.0, g_ref[0, h, i] / l, 0.0)).astype(o_ref.dtype)

    tile_max = []
    for h in range(hb):
        tile_max.append(masked_scores(h, 0, 0))
        p_ref[h] = jnp.zeros((t, t), BF16)
        acc_ref[h] = jnp.zeros(acc_ref.shape[1:], F32)

    def body(_, carry):
        (i, j, i_prev, j_prev), heads = carry
        row_end = j == i
        i_next = jnp.minimum(jnp.where(row_end, i + 1, i), nq - 1)
        j_next = jnp.where(row_end, jnp.maximum(i_next - back, 0), j + 1)
        row_start = j == jnp.maximum(i - back, 0)
        out = []
        for h in range(hb):
            m, alpha, s_max = heads[h]
            acc = acc_ref[h] * alpha + weighted_values(h, j_prev)
            acc_ref[h] = acc
            emit(h, i_prev, acc)
            m = jnp.where(row_start, MASKED, m)
            m_new = jnp.maximum(m, s_max)
            alpha = jnp.exp2(m - m_new)
            p_ref[h] = jnp.exp2(s_ref[h] - m_new).astype(BF16)
            s_max = masked_scores(h, i_next, j_next)
            out.append((m_new, alpha, s_max))
        return (i_next, j_next, i, j), tuple(out)

    zero = jnp.int32(0)
    heads = tuple((jnp.full((1, t), MASKED, F32), jnp.ones((1, t), F32), tile_max[h]) for h in range(hb))
    _, heads = lax.fori_loop(0, n_pairs, body, ((zero, zero, zero, zero), heads))
    for h in range(hb):
        m, alpha, _ = heads[h]
        emit(h, nq - 1, acc_ref[h] * alpha + weighted_values(h, nq - 1))
    for x_ref in extras:
        for h in range(hb):
            for i in range(nq):
                o_ref[0, h, i] = (o_ref[0, h, i].astype(F32) + x_ref[0, h, i].astype(F32)).astype(o_ref.dtype)


def _mask_table(t, n, window):
    k = jnp.arange(t)[:, None]
    q = jnp.arange(t)[None, :]
    tiles = []
    for d in range(n):
        off = (d - (n - 1)) * t
        ok = (k + off) <= q
        if window is not None:
            ok = ok & ((k + off) > q - window)
        tiles.append(ok)
    return jnp.where(jnp.stack(tiles), 0.0, MASKED).astype(F32)


def _flash(q, ka, v_t, gate, aug=None, extras=(), *, hb, window=None):
    B, H, nq, hd, t = q.shape
    G = v_t.shape[1]
    T = nq * t
    assert hb % (H // G) == 0
    kb = hb // (H // G)
    if window is None:
        back = nq - 1
        masks = _mask_table(t, 2, None)
    else:
        assert window % t == 0
        back = window // t
        masks = _mask_table(t, back + 1, window)
    tile = lambda n, c: pl.BlockSpec((1, n, nq, c, t), lambda b, h: (b, h, 0, 0, 0))
    augs = () if aug is None else (aug,)
    aug_specs = [tile(hb if a.shape[1] == H else kb, a.shape[3]) for a in augs]
    return pl.pallas_call(
        functools.partial(_flash_kernel, t=t, back=back, n_aug=len(augs), n_extra=len(extras)),
        grid=(B, H // hb),
        in_specs=[
            tile(hb, hd),
            pl.BlockSpec((1, T, kb * AUG), lambda b, h: (b, 0, h)),
            tile(kb, v_t.shape[3]),
            pl.BlockSpec(masks.shape, lambda b, h: (0, 0, 0)),
            tile(hb, 1),
        ] + aug_specs + [tile(hb, hd)] * len(extras),
        out_specs=tile(hb, hd),
        out_shape=jax.ShapeDtypeStruct((B, H, nq, hd, t), BF16),
        scratch_shapes=[pltpu.VMEM((hb, t, t), F32), pltpu.VMEM((hb, t, t), BF16),
                        pltpu.VMEM((hb, v_t.shape[3], t), F32)],
        compiler_params=_params(("parallel", "parallel")),
        name="flash_window" if window is not None else "flash_causal",
    )(q, ka, v_t, masks, gate, *augs, *extras)


def _compress_kernel(x_ref, w1_ref, pe_ref, w1p_ref, w2k_ref, w2vt_ref, kc_ref, vct_ref):
    half = CMP_STRIDE * HEAD_DIM
    n_chunk = x_ref.shape[2] // CMP_STRIDE
    for kind in range(2):
        w1p = w1p_ref[kind]
        pe = pe_ref[kind].astype(BF16)
        bias = (_dot(pe[:, :half], w1p[:, :CMP_HIDDEN]) + _dot(pe[:, half:], w1p[:, CMP_HIDDEN:]))[0:1]
        ab = jnp.zeros((n_chunk, 2 * NSA_KV_HEADS * CMP_HIDDEN), F32)
        for p in range(CMP_STRIDE):
            x_p = x_ref[0, kind, pl.ds(p, n_chunk, stride=CMP_STRIDE), :].astype(BF16)
            ab = ab + _dot(x_p, w1_ref[kind, p])
        for g in range(NSA_KV_HEADS):
            lo = g * 2 * CMP_HIDDEN
            nxt = pltpu.roll(ab[:, lo + CMP_HIDDEN:lo + 2 * CMP_HIDDEN], n_chunk - 1, 0)
            pre = ab[:, lo:lo + CMP_HIDDEN] + nxt + bias
            hdn = (pre * jax.nn.sigmoid(pre)).astype(BF16)
            if kind == 0:
                kc_ref[0, g] = _dot(hdn, w2k_ref[...]).astype(BF16)
            else:
                vct_ref[0, g] = _dot_nt(w2vt_ref[...], hdn).astype(BF16)


def _compress(xc, w1exp, pe8, w1cat, w2k, w2vt):
    B, _, T, _ = xc.shape
    n_chunk = T // CMP_STRIDE
    full = lambda a: pl.BlockSpec(a.shape, lambda b: (0,) * a.ndim)
    return pl.pallas_call(
        _compress_kernel,
        grid=(B,),
        in_specs=[pl.BlockSpec((1,) + xc.shape[1:], lambda b: (b, 0, 0, 0)),
                  full(w1exp), full(pe8), full(w1cat), full(w2k), full(w2vt)],
        out_specs=[pl.BlockSpec((1, NSA_KV_HEADS, n_chunk, HEAD_DIM), lambda b: (b, 0, 0, 0)),
                   pl.BlockSpec((1, NSA_KV_HEADS, HEAD_DIM, n_chunk), lambda b: (b, 0, 0, 0))],
        out_shape=[jax.ShapeDtypeStruct((B, NSA_KV_HEADS, n_chunk, HEAD_DIM), BF16),
                   jax.ShapeDtypeStruct((B, NSA_KV_HEADS, HEAD_DIM, n_chunk), BF16)],
        compiler_params=_params(("parallel",)),
        name="nsa_compress",
    )(xc, w1exp, pe8, w1cat, w2k, w2vt)


def _cmp_attn_kernel(q_ref, kc_ref, vct_ref, g_ref, o_ref, sb_ref, *, tq, t):
    i = pl.program_id(2)
    n_cmp = kc_ref.shape[2]
    n_sel = sb_ref.shape[3]
    n_io = lax.broadcasted_iota(jnp.int32, (n_cmp, tq), 0)
    t_io = i * tq + lax.broadcasted_iota(jnp.int32, (n_cmp, tq), 1)
    mask = jnp.where(n_io * CMP_STRIDE + (CMP_LEN - 1) <= t_io, 0.0, MASKED)
    any_valid = t_io[0:1] >= CMP_LEN - 1
    kc = kc_ref[0, 0]
    vct = vct_ref[0, 0]
    psum = jnp.zeros((n_cmp, tq), F32)
    for r in range(NSA_REP):
        q = q_ref[0, r * HEAD_DIM:(r + 1) * HEAD_DIM, :]
        s = _dot(kc, q) + mask
        m = jnp.max(s, axis=0, keepdims=True)
        e = jnp.exp2(s - m)
        l = jnp.sum(e, axis=0, keepdims=True)
        p = e * jnp.where(any_valid, 1.0 / l, 0.0)
        psum = psum + p
        o = _dot(vct, p.astype(BF16))
        o = (o * g_ref[0, 0, 0, r:r + 1, :]).astype(BF16)
        for c in range(tq // t):
            o_ref[0, r, c] = o[:, c * t:(c + 1) * t]

    jr = lax.broadcasted_iota(jnp.int32, (n_sel, n_cmp), 0)
    nc = lax.broadcasted_iota(jnp.int32, (n_sel, n_cmp), 1)
    overlap = ((nc * CMP_STRIDE < (jr + 1) * SEL_BLOCK)
               & (nc * CMP_STRIDE + CMP_LEN > jr * SEL_BLOCK)).astype(BF16)
    ph, pm, pl_ = _split3(psum)
    imp = _dot(overlap, ph) + _dot(overlap, pm) + _dot(overlap, pl_)
    j_io = lax.broadcasted_iota(jnp.int32, (n_sel, tq), 0)
    cur = jnp.right_shift(i * tq + lax.broadcasted_iota(jnp.int32, (n_sel, tq), 1),
                          SEL_BLOCK.bit_length() - 1)
    is_cur = j_io == cur
    is_fixed = (j_io == 0) | (j_io == cur - 1)
    imp = jnp.where(is_cur, 2.0 * FORCE_SCORE, jnp.where(is_fixed, FORCE_SCORE, imp))
    imp = jnp.where(j_io <= cur, imp, -1.0)
    grp = 8
    imp_g = [imp[k * grp:(k + 1) * grp] for k in range(n_sel // grp)]
    sub = lax.broadcasted_iota(jnp.int32, (grp, tq), 0)
    cnt_g = [jnp.zeros((grp, tq), F32) for _ in imp_g]
    for jp in range(n_sel):
        row = imp[jp:jp + 1, :]
        for k, x in enumerate(imp_g):
            ge = jnp.where(row >= x, 1.0, 0.0)
            gt = jnp.where(row > x, 1.0, 0.0)
            if k > jp // grp:
                beats = ge
            elif k < jp // grp:
                beats = gt
            else:
                beats = jnp.where(sub > jp % grp, ge, gt)
            cnt_g[k] = cnt_g[k] + beats
    cnt = jnp.concatenate(cnt_g, axis=0)
    sb = jnp.where(cnt < min(SEL_TOPK, n_sel), 0.0, MASKED).astype(BF16)
    for c in range(tq // t):
        sb_ref[0, 0, c] = sb[:, c * t:(c + 1) * t]


def _cmp_attn(nq_t, kc, vct, gate_cmp, n_sel, t, tq=2048):
    B, _, T = nq_t.shape
    G = kc.shape[1]
    rows = NSA_REP * HEAD_DIM
    return pl.pallas_call(
        functools.partial(_cmp_attn_kernel, tq=tq, t=t),
        grid=(B, G, T // tq),
        in_specs=[
            pl.BlockSpec((1, rows, tq), lambda b, g, i: (b, g, i)),
            pl.BlockSpec((1, 1) + kc.shape[2:], lambda b, g, i: (b, g, 0, 0)),
            pl.BlockSpec((1, 1) + vct.shape[2:], lambda b, g, i: (b, g, 0, 0)),
            pl.BlockSpec((1, 1, 1, NSA_REP, tq), lambda b, g, i: (b, 0, g, 0, i)),
        ],
        out_specs=[pl.BlockSpec((1, NSA_REP, tq // t, HEAD_DIM, t), lambda b, g, i: (b, g, i, 0, 0)),
                   pl.BlockSpec((1, 1, tq // t, n_sel, t), lambda b, g, i: (b, g, i, 0, 0))],
        out_shape=[jax.ShapeDtypeStruct((B, G * NSA_REP, T // t, HEAD_DIM, t), BF16),
                   jax.ShapeDtypeStruct((B, G, T // t, n_sel, t), BF16)],
        compiler_params=_params(("parallel", "parallel", "parallel")),
        name="nsa_cmp_attn",
    )(nq_t, kc, vct, gate_cmp)


def _mix_out_kernel(x_ref, of_ref, on_ref, wot_ref, g_ref, o_ref):
    heads, n_tiles, hd, t = of_ref.shape[1:]
    for c in range(n_tiles):
        of = of_ref[0, :, c].reshape(heads * hd, t)
        on = on_ref[0, :, c].reshape(heads * hd, t)
        mix_t = _dot(wot_ref[:, 0:FOX_QKV], of) + _dot(wot_ref[:, FOX_QKV:], on)
        rows = slice(c * t, (c + 1) * t)
        o_ref[0, rows, :] = x_ref[0, rows, :] + _rms(mix_t.T, g_ref[...])


def _mix_out(x, o_fox, o_nsa, wot, g, tm=1024):
    B, T, D = x.shape
    heads, _, hd, t = o_fox.shape[1:]
    row = pl.BlockSpec((1, tm, D), lambda b, i: (b, i, 0))
    tiles = pl.BlockSpec((1, heads, tm // t, hd, t), lambda b, i: (b, 0, i, 0, 0))
    return pl.pallas_call(
        _mix_out_kernel,
        grid=(B, T // tm),
        in_specs=[row, tiles, tiles,
                  pl.BlockSpec(wot.shape, lambda b, i: (0, 0)),
                  pl.BlockSpec((1, D), lambda b, i: (0, 0))],
        out_specs=row,
        out_shape=jax.ShapeDtypeStruct((B, T, D), F32),
        compiler_params=_params(("parallel", "parallel")),
        name="mix_out",
    )(x, o_fox, o_nsa, wot, g)


def _mem_kv_kernel(m_ref, g_ref, w_ref, k_ref, v_ref):
    m = _rms(m_ref[0], g_ref[...]).astype(BF16)
    k_ref[0] = _dot(m, w_ref[:, :D_MODEL]).astype(BF16)
    v_ref[0] = _dot(m, w_ref[:, D_MODEL:]).astype(BF16)


def _mem_kv(mem, g, wkv):
    B, M, D = mem.shape
    blk = pl.BlockSpec((1, M, D), lambda b: (b, 0, 0))
    return pl.pallas_call(
        _mem_kv_kernel,
        grid=(B,),
        in_specs=[blk, pl.BlockSpec((1, D), lambda b: (0, 0)), pl.BlockSpec(wkv.shape, lambda b: (0, 0))],
        out_specs=[blk, blk],
        out_shape=[jax.ShapeDtypeStruct((B, M, D), BF16)] * 2,
        compiler_params=_params(("parallel",)),
        name="mem_kv",
    )(mem, g, wkv)


def _cross_kernel(h_ref, k_ref, v_ref, wq_ref, wo_ref, gpre_ref, gpost_ref, o_ref):
    h = h_ref[0]
    n = _rms(h, gpre_ref[...]).astype(BF16)
    scale = CROSS_HEAD_DIM ** -0.5
    q_all = (_dot(n, wq_ref[...]) * scale).astype(BF16)
    heads = []
    for hh in range(CROSS_HEADS):
        sl = slice(hh * CROSS_HEAD_DIM, (hh + 1) * CROSS_HEAD_DIM)
        s = _dot_nt(q_all[:, sl], k_ref[0, :, sl])
        m = jnp.max(s, axis=-1, keepdims=True)
        p = jnp.exp(s - m)
        l = jnp.sum(p, axis=-1, keepdims=True)
        heads.append((_dot(p.astype(BF16), v_ref[0, :, sl]) / l).astype(BF16))
    out = _dot(jnp.concatenate(heads, axis=-1), wo_ref[...])
    o_ref[0] = h + _rms(out, gpost_ref[...])


def _cross(h, k, v, wq, wo, g_pre, g_post, tm=1024):
    B, T, D = h.shape
    M = k.shape[1]
    row = pl.BlockSpec((1, tm, D), lambda b, i: (b, i, 0))
    kvb = pl.BlockSpec((1, M, D), lambda b, i: (b, 0, 0))
    wsp = pl.BlockSpec((D, D), lambda b, i: (0, 0))
    gsp = pl.BlockSpec((1, D), lambda b, i: (0, 0))
    return pl.pallas_call(
        _cross_kernel,
        grid=(B, T // tm),
        in_specs=[row, kvb, kvb, wsp, wsp, gsp, gsp],
        out_specs=row,
        out_shape=jax.ShapeDtypeStruct((B, T, D), F32),
        compiler_params=_params(("parallel", "parallel")),
        name="mem_cross",
    )(h, k, v, wq, wo, g_pre, g_post)


def _mlp_kernel(h_ref, wu_ref, wd_ref, gpre_ref, gpost_ref, o_ref, *, hc):
    h = h_ref[...]
    n = _rms(h, gpre_ref[...]).astype(BF16)
    acc = jnp.zeros(h.shape, F32)
    for c in range(wu_ref.shape[1] // hc):
        u = jnp.maximum(_dot(n, wu_ref[:, c * hc:(c + 1) * hc]), 0.0)
        acc = acc + _dot((u * u).astype(BF16), wd_ref[c * hc:(c + 1) * hc, :])
    o_ref[...] = h + _rms(acc, gpost_ref[...])


def _mlp(h, wu, wd, g_pre, g_post, tm=1024, hc=512):
    N, D = h.shape
    row = pl.BlockSpec((tm, D), lambda i: (i, 0))
    gsp = pl.BlockSpec((1, D), lambda i: (0, 0))
    once = pl.Buffered(1)
    return pl.pallas_call(
        functools.partial(_mlp_kernel, hc=hc),
        grid=(N // tm,),
        in_specs=[row,
                  pl.BlockSpec(wu.shape, lambda i: (0, 0), pipeline_mode=once),
                  pl.BlockSpec(wd.shape, lambda i: (0, 0), pipeline_mode=once),
                  gsp, gsp],
        out_specs=row,
        out_shape=jax.ShapeDtypeStruct((N, D), F32),
        compiler_params=_params(("parallel",)),
        name="relu2_mlp",
    )(h, wu, wd, g_pre, g_post)


def _layer(h, mem, g_mix_pre, w_in, b_forget, w_ck1, w_ck2, w_cv1, w_cv2, pe_k, pe_v,
           w_mix_out, g_mix_post, g_x_pre, g_mem, w_xq, w_xkv, w_xo, g_x_post,
           g_mlp_pre, w_up, w_down, g_mlp_post):
    B, T, D = h.shape
    H, G, hd = NSA_HEADS, NSA_KV_HEADS, HEAD_DIM
    row = lambda g: g.reshape(1, -1)

    cols = {}
    lo = 0
    for name, size in (("fq", FOX_QKV), ("fk", FOX_QKV), ("fv", FOX_QKV), ("ff", FOX_HEADS), ("nq", NSA_Q),
                       ("kc", NSA_KV), ("vc", NSA_KV), ("ks", NSA_KV), ("vs", NSA_KV), ("kw", NSA_KV),
                       ("vw", NSA_KV), ("ng", NSA_HEADS * N_BRANCH)):
        cols[name] = w_in[:, lo:lo + size]
        lo += size
    ng_branch_major = cols["ng"].reshape(D, NSA_HEADS, N_BRANCH).transpose(0, 2, 1).reshape(D, -1)
    wt = jnp.concatenate([cols[k] for k in ("fq", "fv", "nq", "vs", "vw", "ff")]
                         + [ng_branch_major], axis=1).T.astype(BF16)
    wk = jnp.concatenate([cols[k] for k in ("fk", "kc", "vc", "ks", "kw")], axis=1).astype(BF16)
    half = hd // 2
    inv = ROPE_THETA ** (-jnp.arange(half, dtype=F32) / half)
    ang = inv[:, None] * jnp.arange(T, dtype=F32)[None, :]
    cos_t, sin_t = jnp.cos(ang), jnp.sin(ang)
    cos_k, sin_k = jnp.tile(cos_t.T, (1, 2 * G)), jnp.tile(sin_t.T, (1, 2 * G))

    def slot_placement(heads):
        src = jnp.arange(heads * hd)
        return (jnp.arange(heads * AUG)[None, :] == ((src // hd) * AUG + src % hd)[:, None]).astype(F32)

    d = jnp.arange(G * hd)
    partner = jnp.where(d % hd < half, d + half, d - half)
    rot = (jnp.arange(G * hd)[:, None] == partner[None, :]) * jnp.where(d % hd < half, -1.0, 1.0)[None, :]
    place_rope = jnp.concatenate([slot_placement(G), rot @ slot_placement(G)], axis=0).astype(BF16)

    t = 256
    hb = 8
    nq = T // t
    (fq, k_fox, fv, nq_t, nqr, kcvc, ka_slc, vs, ka_win, vw, lf_t, gt_t, gate_tiles) = _in_proj(
        h, row(g_mix_pre), wt, wk, b_forget.reshape(FOX_HEADS, 1), cos_t, sin_t, cos_k, sin_k,
        place_rope, t)
    gate_h = gate_tiles.reshape(B, N_BRANCH, H, nq, 1, t)

    aug_fox, ka_fox = _decay(lf_t, k_fox, t)
    one_gate = jnp.ones((B, FOX_HEADS, nq, 1, t), F32)
    o_fox = _flash(fq, ka_fox, fv, one_gate, aug_fox, hb=hb)

    hsz = CMP_STRIDE * hd
    w1cat = jnp.stack([jnp.concatenate([w[:hsz], w[hsz:]], axis=1) for w in (w_ck1, w_cv1)]).astype(BF16)
    w1exp = jnp.zeros((2, CMP_STRIDE, G, hd, G, 2, CMP_HIDDEN), F32)
    for kind, w in enumerate((w_ck1, w_cv1)):
        halves = w.reshape(2, CMP_STRIDE, hd, CMP_HIDDEN).transpose(1, 2, 0, 3)
        for g in range(G):
            w1exp = w1exp.at[kind, :, g, :, g].set(halves)
    w1exp = w1exp.reshape(2, CMP_STRIDE, G * hd, G * 2 * CMP_HIDDEN).astype(BF16)
    pe8 = jnp.stack([jnp.broadcast_to(p.reshape(1, -1), (8, CMP_LEN * hd)) for p in (pe_k, pe_v)])
    kc, vct = _compress(kcvc, w1exp, pe8, w1cat, w_ck2.astype(BF16), w_cv2.T.astype(BF16))

    gates = gt_t.reshape(B, N_BRANCH, G, NSA_REP, T)
    n_sel = T // SEL_BLOCK
    o_cmp, selbias = _cmp_attn(nq_t, kc, vct, gates[:, 0:1], n_sel, t)

    o_slc = _flash(nqr, ka_slc, vs, gate_h[:, 1], selbias, hb=hb)
    o_nsa = _flash(nqr, ka_win, vw, gate_h[:, 2], extras=(o_cmp, o_slc), hb=hb, window=WINDOW)

    h = _mix_out(h, o_fox, o_nsa, w_mix_out.T.astype(BF16), row(g_mix_post))

    k_mem, v_mem = _mem_kv(mem, row(g_mem), w_xkv.astype(BF16))
    h = _cross(h, k_mem, v_mem, w_xq.astype(BF16), w_xo.astype(BF16), row(g_x_pre), row(g_x_post))

    h = _mlp(h.reshape(B * T, D), w_up.astype(BF16), w_down.astype(BF16),
             row(g_mlp_pre), row(g_mlp_post)).reshape(B, T, D)
    return h


def kernel(x, mem, g_mix_pre, w_in, b_forget, w_ck1, w_ck2, w_cv1, w_cv2, pe_k, pe_v, w_mix_out, g_mix_post,
           g_x_pre, g_mem, w_xq, w_xkv, w_xo, g_x_post, g_mlp_pre, w_up, w_down, g_mlp_post):
    h = x
    for l in range(g_mix_pre.shape[0]):
        h = _layer(h, mem, g_mix_pre[l], w_in[l], b_forget[l], w_ck1[l], w_ck2[l], w_cv1[l], w_cv2[l],
                   pe_k[l], pe_v[l], w_mix_out[l], g_mix_post[l], g_x_pre[l], g_mem[l], w_xq[l], w_xkv[l],
                   w_xo[l], g_x_post[l], g_mlp_pre[l], w_up[l], w_down[l], g_mlp_post[l])
    return h
```

```python
import functools
import math

import jax
import jax.numpy as jnp
from jax import lax
from jax.experimental import pallas as pl
from jax.experimental.pallas import tpu as pltpu

D_MODEL = 1024
HEAD_DIM = 64
FOX_HEADS = 8
NSA_HEADS = 8
NSA_KV_HEADS = 2
NSA_REP = NSA_HEADS // NSA_KV_HEADS
CMP_LEN = 32
CMP_STRIDE = 16
CMP_HIDDEN = 2 * HEAD_DIM
SEL_BLOCK = 64
SEL_TOPK = 16
WINDOW = 512
N_BRANCH = 3
CROSS_HEADS = 4
CROSS_HEAD_DIM = D_MODEL // CROSS_HEADS
MLP_HIDDEN = 4 * D_MODEL
ROPE_THETA = 10000.0
RMS_EPS = 1e-6
FORCE_SCORE = 1e4
MASKED = -1e30
LOG2E = math.log2(math.e)

FOX_QKV = FOX_HEADS * HEAD_DIM
NSA_Q = NSA_HEADS * HEAD_DIM
NSA_KV = NSA_KV_HEADS * HEAD_DIM
AUG = 128

V7X_VMEM_LIMIT = 56 * 1024 * 1024

F32 = jnp.float32
BF16 = jnp.bfloat16


def _params(sem, vmem=V7X_VMEM_LIMIT):
    return pltpu.CompilerParams(dimension_semantics=sem, vmem_limit_bytes=vmem)


def _rms(x, g):
    return x * lax.rsqrt(jnp.mean(x * x, axis=-1, keepdims=True) + RMS_EPS) * g


def _dot(a, b):
    return jnp.dot(a, b, preferred_element_type=F32)


def _dot_nt(a, b):
    return lax.dot_general(a, b, (((1,), (1,)), ((), ())), preferred_element_type=F32)


def _split3(x):
    hi = x.astype(BF16)
    r1 = x - hi.astype(F32)
    mid = r1.astype(BF16)
    lo = (r1 - mid.astype(F32)).astype(BF16)
    return hi, mid, lo


_R_FQ, _R_FV, _R_NQ, _R_VS, _R_VW, _R_FF, _R_NG, _R_END = 0, 512, 1024, 1536, 1664, 1792, 1800, 1824
_C_FK, _C_KC, _C_KS, _C_KW, _C_END = 0, 512, 768, 896, 1024
V_ROWS = HEAD_DIM + 16


def _in_proj_kernel(x_ref, g_ref, wt_ref, wk_ref, bf_ref, cos_ref, sin_ref, cosk_ref, sink_ref, prope_ref,
                    fq_ref, kfox_ref, fv_ref, nq_ref, nqr_ref, kcvc_ref,
                    kslc_ref, vs_ref, kwin_ref, vw_ref, lf_ref, gt_ref, gtt_ref, *, t):
    n = _rms(x_ref[0], g_ref[...]).astype(BF16)
    tm = n.shape[0]
    cos = cos_ref[...]
    sin = sin_ref[...]
    half = HEAD_DIM // 2

    def proj(lo, hi):
        return _dot_nt(wt_ref[lo:hi, :], n)

    def rope(r):
        parts = []
        for h in range(r.shape[0] // HEAD_DIM):
            x1 = r[h * HEAD_DIM:h * HEAD_DIM + half]
            x2 = r[h * HEAD_DIM + half:(h + 1) * HEAD_DIM]
            parts += [x1 * cos - x2 * sin, x2 * cos + x1 * sin]
        return jnp.concatenate(parts, axis=0)

    def store_tiles(r, out_ref):
        r = r.astype(BF16)
        rows = out_ref.shape[3]
        if rows > HEAD_DIM:
            sub = lax.broadcasted_iota(jnp.int32, (rows - HEAD_DIM, t), 0)
            tail = jnp.where(sub == 0, 1.0, 0.0).astype(BF16)
        for h in range(r.shape[0] // HEAD_DIM):
            for c in range(tm // t):
                tile = r[h * HEAD_DIM:(h + 1) * HEAD_DIM, c * t:(c + 1) * t]
                out_ref[0, h, c] = tile if rows == HEAD_DIM else jnp.concatenate([tile, tail], axis=0)

    scale = HEAD_DIM ** -0.5 * LOG2E
    store_tiles(proj(_R_FQ, _R_FV) * scale, fq_ref)
    store_tiles(proj(_R_FV, _R_NQ), fv_ref)
    nq = proj(_R_NQ, _R_VS) * scale
    nq_ref[0] = nq.astype(BF16)
    store_tiles(rope(nq), nqr_ref)
    store_tiles(proj(_R_VS, _R_VW), vs_ref)
    store_tiles(proj(_R_VW, _R_FF), vw_ref)
    small = proj(_R_FF, _R_END)
    z = small[0:FOX_HEADS] + bf_ref[...]
    lf_ref[0] = jnp.minimum(z, 0.0) - jnp.log1p(jnp.exp(-jnp.abs(z)))
    gates = jax.nn.sigmoid(small[FOX_HEADS:])
    gt_ref[0] = gates
    for r in range(gates.shape[0]):
        for c in range(tm // t):
            gtt_ref[0, r, c] = gates[r:r + 1, c * t:(c + 1) * t]

    tok = _dot(n, wk_ref[...])
    kfox_ref[0] = tok[:, _C_FK:_C_KC].astype(BF16)
    kcvc_ref[0, 0] = tok[:, _C_KC:_C_KC + NSA_KV]
    kcvc_ref[0, 1] = tok[:, _C_KC + NSA_KV:_C_KS]

    def rope_placed(k):
        both = jnp.concatenate([(k * cosk_ref[...]).astype(BF16), (k * sink_ref[...]).astype(BF16)], axis=1)
        return _dot(both, prope_ref[...])

    lane = lax.broadcasted_iota(jnp.int32, (tm, NSA_KV_HEADS * AUG), 1) & (AUG - 1)
    pos = pl.program_id(1) * tm + lax.broadcasted_iota(jnp.int32, (tm, NSA_KV_HEADS * AUG), 0)
    block_col = lane == HEAD_DIM + jnp.right_shift(pos, SEL_BLOCK.bit_length() - 1)
    kslc_ref[0] = (rope_placed(tok[:, _C_KS:_C_KW]) + jnp.where(block_col, 1.0, 0.0)).astype(BF16)
    kwin_ref[0] = rope_placed(tok[:, _C_KW:_C_END]).astype(BF16)


def _in_proj(x, g, wt, wk, b_forget, cos_t, sin_t, cos_k, sin_k, place_rope, t, tm=1024):
    B, T, D = x.shape
    grid = (B, T // tm)
    chan = lambda c, dt: (jax.ShapeDtypeStruct((B, c, T), dt), pl.BlockSpec((1, c, tm), lambda b, i: (b, 0, i)))
    toks = lambda c: (jax.ShapeDtypeStruct((B, T, c), BF16), pl.BlockSpec((1, tm, c), lambda b, i: (b, i, 0)))
    tiles = lambda h, rows, dt=BF16: (jax.ShapeDtypeStruct((B, h, T // t, rows, t), dt),
                                      pl.BlockSpec((1, h, tm // t, rows, t), lambda b, i: (b, 0, i, 0, 0)))
    n_gate = NSA_HEADS * N_BRANCH
    outs = [tiles(FOX_HEADS, HEAD_DIM), toks(FOX_QKV), tiles(FOX_HEADS, V_ROWS), chan(NSA_Q, BF16),
            tiles(NSA_HEADS, HEAD_DIM),
            (jax.ShapeDtypeStruct((B, 2, T, NSA_KV), F32), pl.BlockSpec((1, 2, tm, NSA_KV), lambda b, i: (b, 0, i, 0))),
            toks(NSA_KV_HEADS * AUG), tiles(NSA_KV_HEADS, V_ROWS), toks(NSA_KV_HEADS * AUG),
            tiles(NSA_KV_HEADS, V_ROWS), chan(FOX_HEADS, F32), chan(n_gate, F32), tiles(n_gate, 1, F32)]
    full = lambda a: pl.BlockSpec(a.shape, lambda b, i: (0,) * a.ndim)
    return pl.pallas_call(
        functools.partial(_in_proj_kernel, t=t),
        grid=grid,
        in_specs=[
            pl.BlockSpec((1, tm, D), lambda b, i: (b, i, 0)),
            full(g), full(wt), full(wk), full(b_forget),
            pl.BlockSpec((HEAD_DIM // 2, tm), lambda b, i: (0, i)),
            pl.BlockSpec((HEAD_DIM // 2, tm), lambda b, i: (0, i)),
            pl.BlockSpec((tm, NSA_KV), lambda b, i: (i, 0)),
            pl.BlockSpec((tm, NSA_KV), lambda b, i: (i, 0)),
            full(place_rope),
        ],
        out_specs=[o[1] for o in outs],
        out_shape=[o[0] for o in outs],
        compiler_params=_params(("parallel", "parallel")),
        name="in_proj",
    )(x, g, wt, wk, b_forget, cos_t, sin_t, cos_k, sin_k, place_rope)


_CS = 128
Q_AUG_FOX = 16


def _decay_kernel(x_ref, k_ref, qa_ref, ka_ref, *, t):
    x = x_ref[0]
    H, T = x.shape
    r = lax.broadcasted_iota(jnp.int32, (_CS, _CS), 0)
    c = lax.broadcasted_iota(jnp.int32, (_CS, _CS), 1)
    tri = (r <= c).astype(BF16)
    within = []
    for ch in range(T // _CS):
        xh, xm, xl = _split3(x[:, ch * _CS:(ch + 1) * _CS])
        within.append(_dot(xh, tri) + _dot(xm, tri) + _dot(xl, tri))
    carry = jnp.zeros((H, 1), F32)
    chunks = []
    for y in within:
        chunks.append(y + carry)
        carry = carry + y[:, _CS - 1:_CS]
    cs = jnp.concatenate(chunks, axis=1) * LOG2E
    hi, mid, lo = _split3(cs)
    parts = jnp.concatenate([hi.astype(F32), mid.astype(F32), lo.astype(F32), jnp.zeros((H, T), F32)], axis=0)

    n_q = H * Q_AUG_FOX
    row = lax.broadcasted_iota(jnp.int32, (n_q, 4 * H), 0)
    src = lax.broadcasted_iota(jnp.int32, (n_q, 4 * H), 1)
    sh = Q_AUG_FOX.bit_length() - 1
    place_q = ((src == (row & (Q_AUG_FOX - 1)) * H + jnp.right_shift(row, sh)) & ((row & (Q_AUG_FOX - 1)) < 3))
    rq = lax.broadcasted_iota(jnp.int32, (n_q, T), 0) & (Q_AUG_FOX - 1)
    qa = _dot(place_q.astype(BF16), parts.astype(BF16)) + jnp.where((rq >= 3) & (rq < 6), 1.0, 0.0)
    qa = qa.astype(BF16)
    for h in range(H):
        for i in range(T // t):
            qa_ref[0, h, i] = qa[h * Q_AUG_FOX:(h + 1) * Q_AUG_FOX, i * t:(i + 1) * t]

    n_k = H * HEAD_DIM
    src = lax.broadcasted_iota(jnp.int32, (4 * H, n_k), 0)
    col = lax.broadcasted_iota(jnp.int32, (4 * H, n_k), 1)
    ck = col & (HEAD_DIM - 1)
    hk = jnp.right_shift(col, HEAD_DIM.bit_length() - 1)
    place_k = jnp.where((src == (ck - 3) * H + hk) & (ck >= 3) & (ck < 6), -1.0, 0.0).astype(BF16)
    ckt = lax.broadcasted_iota(jnp.int32, (t, n_k), 1) & (HEAD_DIM - 1)
    ones_k = jnp.where(ckt < 3, 1.0, 0.0)
    parts_t = parts.T.astype(BF16)
    for i in range(T // t):
        rows = slice(i * t, (i + 1) * t)
        aug = (_dot(parts_t[rows], place_k) + ones_k).astype(BF16)
        k = k_ref[0, rows, :]
        ka_ref[0, rows, :] = jnp.concatenate(
            [piece for h in range(H) for piece in (k[:, h * HEAD_DIM:(h + 1) * HEAD_DIM],
                                                   aug[:, h * HEAD_DIM:(h + 1) * HEAD_DIM])], axis=1)


def _decay(lf, k, t):
    B, H, T = lf.shape
    return pl.pallas_call(
        functools.partial(_decay_kernel, t=t),
        grid=(B,),
        in_specs=[pl.BlockSpec((1, H, T), lambda b: (b, 0, 0)),
                  pl.BlockSpec((1, T, H * HEAD_DIM), lambda b: (b, 0, 0))],
        out_specs=[pl.BlockSpec((1, H, T // t, Q_AUG_FOX, t), lambda b: (b, 0, 0, 0, 0)),
                   pl.BlockSpec((1, T, H * AUG), lambda b: (b, 0, 0))],
        out_shape=[jax.ShapeDtypeStruct((B, H, T // t, Q_AUG_FOX, t), BF16),
                   jax.ShapeDtypeStruct((B, T, H * AUG), BF16)],
        compiler_params=_params(("parallel",)),
        name="decay_cumsum",
    )(lf, k)


def _flash_kernel(q_ref, k_ref, v_ref, mask_ref, g_ref, *rest, t, back, n_aug, n_extra):
    aug_ref = rest[0] if n_aug else None
    extras = rest[n_aug:n_aug + n_extra]
    o_ref, s_ref, p_ref, acc_ref = rest[n_aug + n_extra:]
    hb, nq, hd = q_ref.shape[1], q_ref.shape[2], q_ref.shape[3]
    rep = hb // v_ref.shape[1]
    n_pairs = sum(min(i, back) + 1 for i in range(nq))

    def masked_scores(h, i, j):
        rows = [q_ref[0, h, i]]
        if aug_ref is not None:
            rows.append(aug_ref[0, h if aug_ref.shape[1] == hb else h // rep, i])
        pad = AUG - sum(r.shape[0] for r in rows)
        q = jnp.concatenate(rows + [jnp.zeros((pad, t), BF16)], axis=0)
        kh = h // rep
        k = k_ref[0, pl.ds(pl.multiple_of(j * t, t), t), kh * AUG:(kh + 1) * AUG]
        s = _dot(k, q) + mask_ref[jnp.maximum(j - i + (mask_ref.shape[0] - 1), 0)]
        s_ref[h] = s
        return jnp.max(s, axis=0, keepdims=True)

    def weighted_values(h, j):
        return _dot(v_ref[0, h // rep, j], p_ref[h])


    tile_max = []
    for h in range(hb):
        tile_max.append(masked_scores(h, 0, 0))
        p_ref[h] = jnp.zeros((t, t), BF16)
        for i in range(nq):
            acc_ref[h, i] = jnp.zeros(acc_ref.shape[2:], F32)

    def body(_, carry):
        (i, j, i_prev, j_prev), heads = carry
        row_end = j == i
        i_next = jnp.minimum(jnp.where(row_end, i + 1, i), nq - 1)
        j_next = jnp.where(row_end, jnp.maximum(i_next - back, 0), j + 1)
        row_start = j == jnp.maximum(i - back, 0)
        out = []
        for h in range(hb):
            m, alpha, s_max = heads[h]
            acc_ref[h, i_prev] = acc_ref[h, i_prev] * alpha + weighted_values(h, j_prev)
            m = jnp.where(row_start, MASKED, m)
            m_new = jnp.maximum(m, s_max)
            alpha = jnp.exp2(m - m_new)
            p_ref[h] = jnp.exp2(s_ref[h] - m_new).astype(BF16)
            s_max = masked_scores(h, i_next, j_next)
            out.append((m_new, alpha, s_max))
        return (i_next, j_next, i, j), tuple(out)

    zero = jnp.int32(0)
    heads = tuple((jnp.full((1, t), MASKED, F32), jnp.ones((1, t), F32), tile_max[h]) for h in range(hb))
    _, heads = lax.fori_loop(0, n_pairs, body, ((zero, zero, zero, zero), heads))
    for h in range(hb):
        m, alpha, _ = heads[h]
        acc_ref[h, nq - 1] = acc_ref[h, nq - 1] * alpha + weighted_values(h, nq - 1)
        for i in range(nq):
            acc = acc_ref[h, i]
            o = acc[:hd] * (g_ref[0, h, i] / acc[hd:hd + 1])
            for x_ref in extras:
                o = o + x_ref[0, h, i].astype(F32)
            o_ref[0, h, i] = o.astype(o_ref.dtype)


def _mask_table(t, n, window):
    k = jnp.arange(t)[:, None]
    q = jnp.arange(t)[None, :]
    tiles = []
    for d in range(n):
        off = (d - (n - 1)) * t
        ok = (k + off) <= q
        if window is not None:
            ok = ok & ((k + off) > q - window)
        tiles.append(ok)
    return jnp.where(jnp.stack(tiles), 0.0, MASKED).astype(F32)


def _flash(q, ka, v_t, gate, aug=None, extras=(), *, hb, window=None):
    B, H, nq, hd, t = q.shape
    G = v_t.shape[1]
    T = nq * t
    assert hb % (H // G) == 0
    kb = hb // (H // G)
    if window is None:
        back = nq - 1
        masks = _mask_table(t, 2, None)
    else:
        assert window % t == 0
        back = window // t
        masks = _mask_table(t, back + 1, window)
    tile = lambda n, c: pl.BlockSpec((1, n, nq, c, t), lambda b, h: (b, h, 0, 0, 0))
    augs = () if aug is None else (aug,)
    aug_specs = [tile(hb if a.shape[1] == H else kb, a.shape[3]) for a in augs]
    return pl.pallas_call(
        functools.partial(_flash_kernel, t=t, back=back, n_aug=len(augs), n_extra=len(extras)),
        grid=(B, H // hb),
        in_specs=[
            tile(hb, hd),
            pl.BlockSpec((1, T, kb * AUG), lambda b, h: (b, 0, h)),
            tile(kb, v_t.shape[3]),
            pl.BlockSpec(masks.shape, lambda b, h: (0, 0, 0)),
            tile(hb, 1),
        ] + aug_specs + [tile(hb, hd)] * len(extras),
        out_specs=tile(hb, hd),
        out_shape=jax.ShapeDtypeStruct((B, H, nq, hd, t), BF16),
        scratch_shapes=[pltpu.VMEM((hb, t, t), F32), pltpu.VMEM((hb, t, t), BF16),
                        pltpu.VMEM((hb, nq, v_t.shape[3], t), F32)],
        compiler_params=_params(("parallel", "parallel")),
        name="flash_window" if window is not None else "flash_causal",
    )(q, ka, v_t, masks, gate, *augs, *extras)


def _compress_kernel(x_ref, w1_ref, pe_ref, w1p_ref, w2k_ref, w2vt_ref, kc_ref, vct_ref):
    half = CMP_STRIDE * HEAD_DIM
    n_chunk = x_ref.shape[2] // CMP_STRIDE
    for kind in range(2):
        w1p = w1p_ref[kind]
        pe = pe_ref[kind].astype(BF16)
        bias = (_dot(pe[:, :half], w1p[:, :CMP_HIDDEN]) + _dot(pe[:, half:], w1p[:, CMP_HIDDEN:]))[0:1]
        ab = jnp.zeros((n_chunk, 2 * NSA_KV_HEADS * CMP_HIDDEN), F32)
        for p in range(CMP_STRIDE):
            x_p = x_ref[0, kind, pl.ds(p, n_chunk, stride=CMP_STRIDE), :].astype(BF16)
            ab = ab + _dot(x_p, w1_ref[kind, p])
        for g in range(NSA_KV_HEADS):
            lo = g * 2 * CMP_HIDDEN
            nxt = pltpu.roll(ab[:, lo + CMP_HIDDEN:lo + 2 * CMP_HIDDEN], n_chunk - 1, 0)
            pre = ab[:, lo:lo + CMP_HIDDEN] + nxt + bias
            hdn = (pre * jax.nn.sigmoid(pre)).astype(BF16)
            if kind == 0:
                kc_ref[0, g] = _dot(hdn, w2k_ref[...]).astype(BF16)
            else:
                vct_ref[0, g] = _dot_nt(w2vt_ref[...], hdn).astype(BF16)


def _compress(xc, w1exp, pe8, w1cat, w2k, w2vt):
    B, _, T, _ = xc.shape
    n_chunk = T // CMP_STRIDE
    full = lambda a: pl.BlockSpec(a.shape, lambda b: (0,) * a.ndim)
    return pl.pallas_call(
        _compress_kernel,
        grid=(B,),
        in_specs=[pl.BlockSpec((1,) + xc.shape[1:], lambda b: (b, 0, 0, 0)),
                  full(w1exp), full(pe8), full(w1cat), full(w2k), full(w2vt)],
        out_specs=[pl.BlockSpec((1, NSA_KV_HEADS, n_chunk, HEAD_DIM), lambda b: (b, 0, 0, 0)),
                   pl.BlockSpec((1, NSA_KV_HEADS, HEAD_DIM, n_chunk), lambda b: (b, 0, 0, 0))],
        out_shape=[jax.ShapeDtypeStruct((B, NSA_KV_HEADS, n_chunk, HEAD_DIM), BF16),
                   jax.ShapeDtypeStruct((B, NSA_KV_HEADS, HEAD_DIM, n_chunk), BF16)],
        compiler_params=_params(("parallel",)),
        name="nsa_compress",
    )(xc, w1exp, pe8, w1cat, w2k, w2vt)


def _cmp_attn_kernel(q_ref, kc_ref, vct_ref, g_ref, o_ref, sb_ref, *, tq, t):
    i = pl.program_id(2)
    n_cmp = kc_ref.shape[2]
    n_sel = sb_ref.shape[3]
    n_io = lax.broadcasted_iota(jnp.int32, (n_cmp, tq), 0)
    t_io = i * tq + lax.broadcasted_iota(jnp.int32, (n_cmp, tq), 1)
    mask = jnp.where(n_io * CMP_STRIDE + (CMP_LEN - 1) <= t_io, 0.0, MASKED)
    any_valid = t_io[0:1] >= CMP_LEN - 1
    kc = kc_ref[0, 0]
    vct = vct_ref[0, 0]
    psum = jnp.zeros((n_cmp, tq), F32)
    for r in range(NSA_REP):
        q = q_ref[0, r * HEAD_DIM:(r + 1) * HEAD_DIM, :]
        s = _dot(kc, q) + mask
        m = jnp.max(s, axis=0, keepdims=True)
        e = jnp.exp2(s - m)
        l = jnp.sum(e, axis=0, keepdims=True)
        p = e * jnp.where(any_valid, 1.0 / l, 0.0)
        psum = psum + p
        o = _dot(vct, p.astype(BF16))
        o = (o * g_ref[0, 0, 0, r:r + 1, :]).astype(BF16)
        for c in range(tq // t):
            o_ref[0, r, c] = o[:, c * t:(c + 1) * t]

    jr = lax.broadcasted_iota(jnp.int32, (n_sel, n_cmp), 0)
    nc = lax.broadcasted_iota(jnp.int32, (n_sel, n_cmp), 1)
    overlap = ((nc * CMP_STRIDE < (jr + 1) * SEL_BLOCK)
               & (nc * CMP_STRIDE + CMP_LEN > jr * SEL_BLOCK)).astype(BF16)
    ph, pm, pl_ = _split3(psum)
    imp = _dot(overlap, ph) + _dot(overlap, pm) + _dot(overlap, pl_)
    j_io = lax.broadcasted_iota(jnp.int32, (n_sel, tq), 0)
    cur = jnp.right_shift(i * tq + lax.broadcasted_iota(jnp.int32, (n_sel, tq), 1),
                          SEL_BLOCK.bit_length() - 1)
    is_cur = j_io == cur
    is_fixed = (j_io == 0) | (j_io == cur - 1)
    imp = jnp.where(is_cur, 2.0 * FORCE_SCORE, jnp.where(is_fixed, FORCE_SCORE, imp))
    imp = jnp.where(j_io <= cur, imp, -1.0)
    grp = 8
    imp_g = [imp[k * grp:(k + 1) * grp] for k in range(n_sel // grp)]
    sub = lax.broadcasted_iota(jnp.int32, (grp, tq), 0)
    cnt_g = [jnp.zeros((grp, tq), F32) for _ in imp_g]
    for jp in range(n_sel):
        row = imp[jp:jp + 1, :]
        for k, x in enumerate(imp_g):
            ge = jnp.where(row >= x, 1.0, 0.0)
            gt = jnp.where(row > x, 1.0, 0.0)
            if k > jp // grp:
                beats = ge
            elif k < jp // grp:
                beats = gt
            else:
                beats = jnp.where(sub > jp % grp, ge, gt)
            cnt_g[k] = cnt_g[k] + beats
    cnt = jnp.concatenate(cnt_g, axis=0)
    sb = jnp.where(cnt < min(SEL_TOPK, n_sel), 0.0, MASKED).astype(BF16)
    for c in range(tq // t):
        sb_ref[0, 0, c] = sb[:, c * t:(c + 1) * t]


def _cmp_attn(nq_t, kc, vct, gate_cmp, n_sel, t, tq=2048):
    B, _, T = nq_t.shape
    G = kc.shape[1]
    rows = NSA_REP * HEAD_DIM
    return pl.pallas_call(
        functools.partial(_cmp_attn_kernel, tq=tq, t=t),
        grid=(B, G, T // tq),
        in_specs=[
            pl.BlockSpec((1, rows, tq), lambda b, g, i: (b, g, i)),
            pl.BlockSpec((1, 1) + kc.shape[2:], lambda b, g, i: (b, g, 0, 0)),
            pl.BlockSpec((1, 1) + vct.shape[2:], lambda b, g, i: (b, g, 0, 0)),
            pl.BlockSpec((1, 1, 1, NSA_REP, tq), lambda b, g, i: (b, 0, g, 0, i)),
        ],
        out_specs=[pl.BlockSpec((1, NSA_REP, tq // t, HEAD_DIM, t), lambda b, g, i: (b, g, i, 0, 0)),
                   pl.BlockSpec((1, 1, tq // t, n_sel, t), lambda b, g, i: (b, g, i, 0, 0))],
        out_shape=[jax.ShapeDtypeStruct((B, G * NSA_REP, T // t, HEAD_DIM, t), BF16),
                   jax.ShapeDtypeStruct((B, G, T // t, n_sel, t), BF16)],
        compiler_params=_params(("parallel", "parallel", "parallel")),
        name="nsa_cmp_attn",
    )(nq_t, kc, vct, gate_cmp)


def _mix_out_kernel(x_ref, of_ref, on_ref, wot_ref, g_ref, o_ref):
    heads, n_tiles, hd, t = of_ref.shape[1:]
    for c in range(n_tiles):
        of = of_ref[0, :, c].reshape(heads * hd, t)
        on = on_ref[0, :, c].reshape(heads * hd, t)
        mix_t = _dot(wot_ref[:, 0:FOX_QKV], of) + _dot(wot_ref[:, FOX_QKV:], on)
        rows = slice(c * t, (c + 1) * t)
        o_ref[0, rows, :] = x_ref[0, rows, :] + _rms(mix_t.T, g_ref[...])


def _mix_out(x, o_fox, o_nsa, wot, g, tm=1024):
    B, T, D = x.shape
    heads, _, hd, t = o_fox.shape[1:]
    row = pl.BlockSpec((1, tm, D), lambda b, i: (b, i, 0))
    tiles = pl.BlockSpec((1, heads, tm // t, hd, t), lambda b, i: (b, 0, i, 0, 0))
    return pl.pallas_call(
        _mix_out_kernel,
        grid=(B, T // tm),
        in_specs=[row, tiles, tiles,
                  pl.BlockSpec(wot.shape, lambda b, i: (0, 0)),
                  pl.BlockSpec((1, D), lambda b, i: (0, 0))],
        out_specs=row,
        out_shape=jax.ShapeDtypeStruct((B, T, D), F32),
        compiler_params=_params(("parallel", "parallel")),
        name="mix_out",
    )(x, o_fox, o_nsa, wot, g)


def _mem_kv_kernel(m_ref, g_ref, w_ref, k_ref, v_ref):
    m = _rms(m_ref[0], g_ref[...]).astype(BF16)
    k_ref[0] = _dot(m, w_ref[:, :D_MODEL]).astype(BF16)
    v_ref[0] = _dot(m, w_ref[:, D_MODEL:]).astype(BF16)


def _mem_kv(mem, g, wkv):
    B, M, D = mem.shape
    blk = pl.BlockSpec((1, M, D), lambda b: (b, 0, 0))
    return pl.pallas_call(
        _mem_kv_kernel,
        grid=(B,),
        in_specs=[blk, pl.BlockSpec((1, D), lambda b: (0, 0)), pl.BlockSpec(wkv.shape, lambda b: (0, 0))],
        out_specs=[blk, blk],
        out_shape=[jax.ShapeDtypeStruct((B, M, D), BF16)] * 2,
        compiler_params=_params(("parallel",)),
        name="mem_kv",
    )(mem, g, wkv)


def _cross_kernel(h_ref, k_ref, v_ref, wq_ref, wo_ref, gpre_ref, gpost_ref, o_ref):
    h = h_ref[0]
    n = _rms(h, gpre_ref[...]).astype(BF16)
    scale = CROSS_HEAD_DIM ** -0.5
    q_all = (_dot(n, wq_ref[...]) * scale).astype(BF16)
    heads = []
    for hh in range(CROSS_HEADS):
        sl = slice(hh * CROSS_HEAD_DIM, (hh + 1) * CROSS_HEAD_DIM)
        s = _dot_nt(q_all[:, sl], k_ref[0, :, sl])
        m = jnp.max(s, axis=-1, keepdims=True)
        p = jnp.exp(s - m)
        l = jnp.sum(p, axis=-1, keepdims=True)
        heads.append((_dot(p.astype(BF16), v_ref[0, :, sl]) / l).astype(BF16))
    out = _dot(jnp.concatenate(heads, axis=-1), wo_ref[...])
    o_ref[0] = h + _rms(out, gpost_ref[...])


def _cross(h, k, v, wq, wo, g_pre, g_post, tm=1024):
    B, T, D = h.shape
    M = k.shape[1]
    row = pl.BlockSpec((1, tm, D), lambda b, i: (b, i, 0))
    kvb = pl.BlockSpec((1, M, D), lambda b, i: (b, 0, 0))
    wsp = pl.BlockSpec((D, D), lambda b, i: (0, 0))
    gsp = pl.BlockSpec((1, D), lambda b, i: (0, 0))
    return pl.pallas_call(
        _cross_kernel,
        grid=(B, T // tm),
        in_specs=[row, kvb, kvb, wsp, wsp, gsp, gsp],
        out_specs=row,
        out_shape=jax.ShapeDtypeStruct((B, T, D), F32),
        compiler_params=_params(("parallel", "parallel")),
        name="mem_cross",
    )(h, k, v, wq, wo, g_pre, g_post)


def _mlp_kernel(h_ref, wu_ref, wd_ref, gpre_ref, gpost_ref, o_ref, *, hc):
    h = h_ref[...]
    n = _rms(h, gpre_ref[...]).astype(BF16)
    acc = jnp.zeros(h.shape, F32)
    for c in range(wu_ref.shape[1] // hc):
        u = jnp.maximum(_dot(n, wu_ref[:, c * hc:(c + 1) * hc]), 0.0)
        acc = acc + _dot((u * u).astype(BF16), wd_ref[c * hc:(c + 1) * hc, :])
    o_ref[...] = h + _rms(acc, gpost_ref[...])


def _mlp(h, wu, wd, g_pre, g_post, tm=1024, hc=512):
    N, D = h.shape
    row = pl.BlockSpec((tm, D), lambda i: (i, 0))
    gsp = pl.BlockSpec((1, D), lambda i: (0, 0))
    once = pl.Buffered(1)
    return pl.pallas_call(
        functools.partial(_mlp_kernel, hc=hc),
        grid=(N // tm,),
        in_specs=[row,
                  pl.BlockSpec(wu.shape, lambda i: (0, 0), pipeline_mode=once),
                  pl.BlockSpec(wd.shape, lambda i: (0, 0), pipeline_mode=once),
                  gsp, gsp],
        out_specs=row,
        out_shape=jax.ShapeDtypeStruct((N, D), F32),
        compiler_params=_params(("parallel",)),
        name="relu2_mlp",
    )(h, wu, wd, g_pre, g_post)


def _layer(h, mem, g_mix_pre, w_in, b_forget, w_ck1, w_ck2, w_cv1, w_cv2, pe_k, pe_v,
           w_mix_out, g_mix_post, g_x_pre, g_mem, w_xq, w_xkv, w_xo, g_x_post,
           g_mlp_pre, w_up, w_down, g_mlp_post):
    B, T, D = h.shape
    H, G, hd = NSA_HEADS, NSA_KV_HEADS, HEAD_DIM
    row = lambda g: g.reshape(1, -1)

    cols = {}
    lo = 0
    for name, size in (("fq", FOX_QKV), ("fk", FOX_QKV), ("fv", FOX_QKV), ("ff", FOX_HEADS), ("nq", NSA_Q),
                       ("kc", NSA_KV), ("vc", NSA_KV), ("ks", NSA_KV), ("vs", NSA_KV), ("kw", NSA_KV),
                       ("vw", NSA_KV), ("ng", NSA_HEADS * N_BRANCH)):
        cols[name] = w_in[:, lo:lo + size]
        lo += size
    ng_branch_major = cols["ng"].reshape(D, NSA_HEADS, N_BRANCH).transpose(0, 2, 1).reshape(D, -1)
    wt = jnp.concatenate([cols[k] for k in ("fq", "fv", "nq", "vs", "vw", "ff")]
                         + [ng_branch_major], axis=1).T.astype(BF16)
    wk = jnp.concatenate([cols[k] for k in ("fk", "kc", "vc", "ks", "kw")], axis=1).astype(BF16)
    half = hd // 2
    inv = ROPE_THETA ** (-jnp.arange(half, dtype=F32) / half)
    ang = inv[:, None] * jnp.arange(T, dtype=F32)[None, :]
    cos_t, sin_t = jnp.cos(ang), jnp.sin(ang)
    cos_k, sin_k = jnp.tile(cos_t.T, (1, 2 * G)), jnp.tile(sin_t.T, (1, 2 * G))

    def slot_placement(heads):
        src = jnp.arange(heads * hd)
        return (jnp.arange(heads * AUG)[None, :] == ((src // hd) * AUG + src % hd)[:, None]).astype(F32)

    d = jnp.arange(G * hd)
    partner = jnp.where(d % hd < half, d + half, d - half)
    rot = (jnp.arange(G * hd)[:, None] == partner[None, :]) * jnp.where(d % hd < half, -1.0, 1.0)[None, :]
    place_rope = jnp.concatenate([slot_placement(G), rot @ slot_placement(G)], axis=0).astype(BF16)

    t = 256
    hb = 8
    nq = T // t
    (fq, k_fox, fv, nq_t, nqr, kcvc, ka_slc, vs, ka_win, vw, lf_t, gt_t, gate_tiles) = _in_proj(
        h, row(g_mix_pre), wt, wk, b_forget.reshape(FOX_HEADS, 1), cos_t, sin_t, cos_k, sin_k,
        place_rope, t)
    gate_h = gate_tiles.reshape(B, N_BRANCH, H, nq, 1, t)

    aug_fox, ka_fox = _decay(lf_t, k_fox, t)
    one_gate = jnp.ones((B, FOX_HEADS, nq, 1, t), F32)
    o_fox = _flash(fq, ka_fox, fv, one_gate, aug_fox, hb=hb)

    hsz = CMP_STRIDE * hd
    w1cat = jnp.stack([jnp.concatenate([w[:hsz], w[hsz:]], axis=1) for w in (w_ck1, w_cv1)]).astype(BF16)
    w1exp = jnp.zeros((2, CMP_STRIDE, G, hd, G, 2, CMP_HIDDEN), F32)
    for kind, w in enumerate((w_ck1, w_cv1)):
        halves = w.reshape(2, CMP_STRIDE, hd, CMP_HIDDEN).transpose(1, 2, 0, 3)
        for g in range(G):
            w1exp = w1exp.at[kind, :, g, :, g].set(halves)
    w1exp = w1exp.reshape(2, CMP_STRIDE, G * hd, G * 2 * CMP_HIDDEN).astype(BF16)
    pe8 = jnp.stack([jnp.broadcast_to(p.reshape(1, -1), (8, CMP_LEN * hd)) for p in (pe_k, pe_v)])
    kc, vct = _compress(kcvc, w1exp, pe8, w1cat, w_ck2.astype(BF16), w_cv2.T.astype(BF16))

    gates = gt_t.reshape(B, N_BRANCH, G, NSA_REP, T)
    n_sel = T // SEL_BLOCK
    o_cmp, selbias = _cmp_attn(nq_t, kc, vct, gates[:, 0:1], n_sel, t)

    o_slc = _flash(nqr, ka_slc, vs, gate_h[:, 1], selbias, hb=hb)
    o_nsa = _flash(nqr, ka_win, vw, gate_h[:, 2], extras=(o_cmp, o_slc), hb=hb, window=WINDOW)

    h = _mix_out(h, o_fox, o_nsa, w_mix_out.T.astype(BF16), row(g_mix_post))

    k_mem, v_mem = _mem_kv(mem, row(g_mem), w_xkv.astype(BF16))
    h = _cross(h, k_mem, v_mem, w_xq.astype(BF16), w_xo.astype(BF16), row(g_x_pre), row(g_x_post))

    h = _mlp(h.reshape(B * T, D), w_up.astype(BF16), w_down.astype(BF16),
             row(g_mlp_pre), row(g_mlp_post)).reshape(B, T, D)
    return h


def kernel(x, mem, g_mix_pre, w_in, b_forget, w_ck1, w_ck2, w_cv1, w_cv2, pe_k, pe_v, w_mix_out, g_mix_post,
           g_x_pre, g_mem, w_xq, w_xkv, w_xo, g_x_post, g_mlp_pre, w_up, w_down, g_mlp_post):
    h = x
    for l in range(g_mix_pre.shape[0]):
        h = _layer(h, mem, g_mix_pre[l], w_in[l], b_forget[l], w_ck1[l], w_ck2[l], w_cv1[l], w_cv2[l],
                   pe_k[l], pe_v[l], w_mix_out[l], g_mix_post[l], g_x_pre[l], g_mem[l], w_xq[l], w_xkv[l],
                   w_xo[l], g_x_post[l], g_mlp_pre[l], w_up[l], w_down[l], g_mlp_post[l])
    return h
```
